```python
import jax, jax.numpy as jnp
from jax import lax
import numpy as np

D_MODEL = 1024
BATCH = 2
SEQ = 8192
DEPTH = 4
DEC_BATCH = 128
DEC_SEQ = 8
PAST_LEN = 8192
PAGE_SIZE = 128

N_A_LAYERS = DEPTH // 2
N_B_LAYERS = DEPTH - N_A_LAYERS
CHUNK = 128
SGU_WIDTH = D_MODEL
SGU_GROUPS = 8
SGU_GROUP_DIM = SGU_WIDTH // SGU_GROUPS
HEAD_DIM = 64
N_Q_HEADS = D_MODEL // HEAD_DIM
N_KV_HEADS = 2
Q_PER_KV = N_Q_HEADS // N_KV_HEADS
WINDOW = 128
D_FF = -(-8 * D_MODEL // (3 * 256)) * 256
RMS_EPS = 1e-6
LN_EPS = 1e-5

kernel_name = "yoco_gmlp_swa_sink_decoder_step"


def rms_norm(x, g):
    xf = x.astype(jnp.float32)
    y = xf * lax.rsqrt(jnp.mean(xf * xf, axis=-1, keepdims=True) + RMS_EPS)
    return (y * g.astype(jnp.float32)).astype(x.dtype)


def layer_norm(x, g, b):
    xf = x.astype(jnp.float32)
    mu = jnp.mean(xf, axis=-1, keepdims=True)
    var = jnp.mean(jnp.square(xf - mu), axis=-1, keepdims=True)
    y = (xf - mu) * lax.rsqrt(var + LN_EPS)
    return (y * g.astype(jnp.float32) + b.astype(jnp.float32)).astype(x.dtype)


def swiglu_ffn(h, w_gate, w_up, w_down):
    return (jax.nn.silu(h @ w_gate) * (h @ w_up)) @ w_down


def gmlp_mixer(h, n, w_in, ln_g, ln_b, w_s, b_s, w_out):
    B, S, _ = h.shape
    z = jax.nn.gelu(h @ w_in, approximate=False)
    u, v = jnp.split(z, 2, axis=-1)
    v = layer_norm(v, ln_g, ln_b)
    vc = v.reshape(B, S // n, n, SGU_GROUPS, SGU_GROUP_DIM)
    causal = jnp.tril(jnp.ones((n, n), dtype=bool))
    w = jnp.where(causal, w_s[:, :n, :n], 0.0)
    mixed = jnp.einsum('gts,bcsgd->bctgd', w, vc) + b_s[:, :n].T[:, :, None]
    y = u * mixed.reshape(B, S, SGU_WIDTH)
    v_rows = vc[:, -1].reshape(B, n, SGU_WIDTH)
    return y @ w_out, v_rows


def sink_softmax(scores, sink):
    s = jnp.broadcast_to(sink.astype(jnp.float32)[:, :, None, None], scores.shape[:-1] + (1,))
    probs = jax.nn.softmax(jnp.concatenate([scores, s], axis=-1), axis=-1)
    return probs[..., :-1]


def swa_prompt(q, k, v, sinks):
    B, S = q.shape[:2]
    nb = S // WINDOW
    qb = q.reshape(B, nb, WINDOW, N_KV_HEADS, Q_PER_KV, HEAD_DIM)
    kb = k.reshape(B, nb, WINDOW, N_KV_HEADS, HEAD_DIM)
    vb = v.reshape(B, nb, WINDOW, N_KV_HEADS, HEAD_DIM)
    pad = ((0, 0), (1, 0), (0, 0), (0, 0), (0, 0))
    kk = jnp.concatenate([jnp.pad(kb, pad)[:, :-1], kb], axis=2)
    vv = jnp.concatenate([jnp.pad(vb, pad)[:, :-1], vb], axis=2)
    qpos = jnp.arange(WINDOW)[:, None] + WINDOW
    kpos = jnp.arange(2 * WINDOW)[None, :]
    band = (kpos <= qpos) & (qpos - kpos < WINDOW)
    blk = jnp.arange(nb)[:, None, None]
    valid = band[None] & ((blk > 0) | (kpos[None] >= WINDOW))
    scores = jnp.einsum('bnqkgd,bnskd->bnkgqs', qb, kk,
                        preferred_element_type=jnp.float32) * (HEAD_DIM ** -0.5)
    scores = jnp.where(valid[None, :, None, None], scores, -jnp.inf)
    probs = sink_softmax(scores, sinks.reshape(N_KV_HEADS, Q_PER_KV))
    out = jnp.einsum('bnkgqs,bnskd->bnqkgd', probs.astype(vv.dtype), vv)
    return out.reshape(B, S, N_Q_HEADS * HEAD_DIM)


def swa_sample(q, k_new, v_new, cache_k, cache_v, sinks):
    DB, T = q.shape[:2]
    cw = cache_k.shape[1]
    kk = jnp.concatenate([cache_k.astype(k_new.dtype), k_new], axis=1)
    vv = jnp.concatenate([cache_v.astype(v_new.dtype), v_new], axis=1)
    qpos = PAST_LEN + jnp.arange(T)
    kpos = jnp.concatenate([PAST_LEN - cw + jnp.arange(cw), PAST_LEN + jnp.arange(T)])
    valid = (kpos[None, :] <= qpos[:, None]) & (qpos[:, None] - kpos[None, :] < WINDOW)
    qg = q.reshape(DB, T, N_KV_HEADS, Q_PER_KV, HEAD_DIM)
    scores = jnp.einsum('btkgd,bskd->bkgts', qg, kk,
                        preferred_element_type=jnp.float32) * (HEAD_DIM ** -0.5)
    scores = jnp.where(valid, scores, -jnp.inf)
    probs = sink_softmax(scores, sinks.reshape(N_KV_HEADS, Q_PER_KV))
    out = jnp.einsum('bkgts,bskd->btkgd', probs.astype(vv.dtype), vv)
    return out.reshape(DB, T, N_Q_HEADS * HEAD_DIM)


def trunk(x, is_prompt, cache_k, cache_v, sg_norm_pre, sg_w_in, sg_ln_g, sg_ln_b, sg_w_s, sg_b_s,
          sg_w_out, sg_norm_post, kv_norm, w_kv, b_kv, sw_norm_pre, sw_w_q, sw_b_q, sw_sinks,
          sw_w_o, sw_b_o, sw_norm_post, f_norm_pre, f_w_gate, f_w_up, f_w_down, f_norm_post):
    B, S, _ = x.shape
    n = CHUNK if is_prompt else S
    v_rows = []
    k_sh = v_sh = None
    for layer in range(DEPTH):
        if layer < N_A_LAYERS:
            i = layer
            h = rms_norm(x, sg_norm_pre[i])
            out, vr = gmlp_mixer(h, n, sg_w_in[i], sg_ln_g[i], sg_ln_b[i], sg_w_s[i], sg_b_s[i], sg_w_out[i])
            v_rows.append(vr)
        else:
            i = layer - N_A_LAYERS
            if i == 0:
                kv = rms_norm(x, kv_norm) @ w_kv + b_kv
                k_sh, v_sh = jnp.split(kv.reshape(B, S, 2 * N_KV_HEADS, HEAD_DIM), 2, axis=2)
            h = rms_norm(x, sw_norm_pre[i])
            q = (h @ sw_w_q[i] + sw_b_q[i]).reshape(B, S, N_Q_HEADS, HEAD_DIM)
            if is_prompt:
                att = swa_prompt(q, k_sh, v_sh, sw_sinks[i])
            else:
                att = swa_sample(q, k_sh, v_sh, cache_k, cache_v, sw_sinks[i])
            out = att @ sw_w_o[i] + sw_b_o[i]
            norm_post = sw_norm_post[i]
        x = x + rms_norm(out, sg_norm_post[layer] if layer < N_A_LAYERS else sw_norm_post[layer - N_A_LAYERS])
        h = rms_norm(x, f_norm_pre[layer])
        x = x + rms_norm(swiglu_ffn(h, f_w_gate[layer], f_w_up[layer], f_w_down[layer]), f_norm_post[layer])
    return x, jnp.stack(v_rows, axis=0), k_sh, v_sh


def setup_inputs(seed: int = 0) -> dict:
    key = jax.random.key(seed)
    ks = iter(jax.random.split(key, 40))
    nrm = lambda shape, scale: jax.random.normal(next(ks), shape, jnp.float32) * scale
    cache_rows = min(WINDOW, PAST_LEN)
    qw = N_Q_HEADS * HEAD_DIM
    kvw = N_KV_HEADS * HEAD_DIM
    na, nb_, d = N_A_LAYERS, N_B_LAYERS, D_MODEL
    return {
        'x_prompt': nrm((BATCH, SEQ, d), 1.0),
        'x_sample': nrm((DEC_BATCH, DEC_SEQ, d), 1.0),
        'cache_k': nrm((DEC_BATCH, cache_rows, N_KV_HEADS, HEAD_DIM), 1.0),
        'cache_v': nrm((DEC_BATCH, cache_rows, N_KV_HEADS, HEAD_DIM), 1.0),
        'sg_norm_pre': 1.0 + nrm((na, d), 0.05),
        'sg_w_in': nrm((na, d, 2 * SGU_WIDTH), d ** -0.5),
        'sg_ln_g': 1.0 + nrm((na, SGU_WIDTH), 0.05),
        'sg_ln_b': nrm((na, SGU_WIDTH), 0.02),
        'sg_w_s': nrm((na, SGU_GROUPS, CHUNK, CHUNK), 0.5 * CHUNK ** -0.5),
        'sg_b_s': 1.0 + nrm((na, SGU_GROUPS, CHUNK), 0.1),
        'sg_w_out': nrm((na, SGU_WIDTH, d), SGU_WIDTH ** -0.5),
        'sg_norm_post': 1.0 + nrm((na, d), 0.05),
        'kv_norm': 1.0 + nrm((d,), 0.05),
        'w_kv': nrm((d, 2 * kvw), d ** -0.5),
        'b_kv': nrm((2 * kvw,), 0.02),
        'sw_norm_pre': 1.0 + nrm((nb_, d), 0.05),
        'sw_w_q': nrm((nb_, d, qw), d ** -0.5),
        'sw_b_q': nrm((nb_, qw), 0.02),
        'sw_sinks': nrm((nb_, N_Q_HEADS), 0.5),
        'sw_w_o': nrm((nb_, qw, d), qw ** -0.5),
        'sw_b_o': nrm((nb_, d), 0.02),
        'sw_norm_post': 1.0 + nrm((nb_, d), 0.05),
        'f_norm_pre': 1.0 + nrm((DEPTH, d), 0.05),
        'f_w_gate': nrm((DEPTH, d, D_FF), d ** -0.5),
        'f_w_up': nrm((DEPTH, d, D_FF), d ** -0.5),
        'f_w_down': nrm((DEPTH, D_FF, d), D_FF ** -0.5),
        'f_norm_post': 1.0 + nrm((DEPTH, d), 0.05),
    }


def reference(x_prompt, x_sample, cache_k, cache_v, sg_norm_pre, sg_w_in, sg_ln_g, sg_ln_b, sg_w_s,
              sg_b_s, sg_w_out, sg_norm_post, kv_norm, w_kv, b_kv, sw_norm_pre, sw_w_q, sw_b_q,
              sw_sinks, sw_w_o, sw_b_o, sw_norm_post, f_norm_pre, f_w_gate, f_w_up, f_w_down,
              f_norm_post):
    weights = (sg_norm_pre, sg_w_in, sg_ln_g, sg_ln_b, sg_w_s, sg_b_s, sg_w_out, sg_norm_post,
               kv_norm, w_kv, b_kv, sw_norm_pre, sw_w_q, sw_b_q, sw_sinks, sw_w_o, sw_b_o,
               sw_norm_post, f_norm_pre, f_w_gate, f_w_up, f_w_down, f_norm_post)
    y_prompt, v_rows_prompt, k_p, v_p = trunk(x_prompt, True, None, None, *weights)
    y_sample, v_rows_sample, k_s, v_s = trunk(x_sample, False, cache_k, cache_v, *weights)
    k_prompt = k_p[:, -WINDOW:]
    v_prompt = v_p[:, -WINDOW:]
    return (y_prompt, y_sample, v_rows_prompt, v_rows_sample, k_prompt, v_prompt, k_s, v_s)
```

```python
import functools

import jax
import jax.numpy as jnp
from jax import lax
from jax.experimental import pallas as pl
from jax.experimental.pallas import tpu as pltpu

F32 = jnp.float32
BF16 = jnp.bfloat16

RMS_EPS = 1e-6
LN_EPS = 1e-5
CHUNK = 128
LANES = 128
HEAD_DIM = 64
N_KV_HEADS = 2
HEADS_PER_KV = 8
PAIRS_PER_KV = HEADS_PER_KV // 2
ROW_TILE = 512
VMEM_LIMIT = 56 * 1024 * 1024
NEG_INF = float("-inf")


def _rms(x, g):
    ms = jnp.mean(x * x, axis=-1, keepdims=True)
    return x * lax.rsqrt(ms + RMS_EPS) * g


def _const_spec(shape):
    zeros = (0,) * len(shape)
    return pl.BlockSpec(shape, lambda i: zeros, pipeline_mode=pl.Buffered(1))


def _params():
    return pltpu.CompilerParams(dimension_semantics=("arbitrary",), vmem_limit_bytes=VMEM_LIMIT)


def _mixer_kernel(x_ref, gpre_ref, win_ref, lng_ref, lnb_ref, wmix_ref, bmix_ref, wout_ref, gpost_ref,
                  xo_ref, vsp_ref, vss_ref, y_s, *, n_prompt_tiles, tiles_per_batch):
    i = pl.program_id(0)
    tm, width = y_s.shape
    x = x_ref[...]
    h = _rms(x, gpre_ref[...]).astype(BF16)
    z = jnp.dot(h, win_ref[...], preferred_element_type=F32)
    z = 0.5 * z * (1.0 + lax.erf(z * (0.5 ** 0.5)))
    u = z[:, :width]
    v = z[:, width:]
    mu = jnp.mean(v, axis=-1, keepdims=True)
    vc = v - mu
    var = jnp.mean(vc * vc, axis=-1, keepdims=True)
    v = vc * lax.rsqrt(var + LN_EPS) * lng_ref[...] + lnb_ref[...]

    @pl.when(jnp.logical_and(i < n_prompt_tiles, i % tiles_per_batch == tiles_per_batch - 1))
    def _():
        vsp_ref[0] = v[tm - CHUNK:, :]

    @pl.when(i >= n_prompt_tiles)
    def _():
        vss_ref[...] = v

    vb = v.astype(BF16)
    bias = bmix_ref[0]
    for c in range(tm // CHUNK):
        rows = slice(c * CHUNK, (c + 1) * CHUNK)
        for g in range(width // LANES):
            cols = slice(g * LANES, (g + 1) * LANES)
            mixed = jnp.dot(wmix_ref[0, g], vb[rows, cols], preferred_element_type=F32)
            y_s[rows, cols] = (u[rows, cols] * (mixed + bias[:, cols])).astype(BF16)
    out = jnp.dot(y_s[...], wout_ref[...], preferred_element_type=F32)
    xo_ref[...] = x + _rms(out, gpost_ref[...])


def _mixer(x, gpre, win, lng, lnb, wmix, bmix, wout, gpost, *, n_prompt_rows, prompt_batch):
    rows, d = x.shape
    width = wout.shape[0]
    tm = ROW_TILE
    npt = n_prompt_rows // tm
    tpb = npt // prompt_batch
    n_sample_rows = rows - n_prompt_rows
    groups = wmix.shape[1]
    kern = functools.partial(_mixer_kernel, n_prompt_tiles=npt, tiles_per_batch=tpb)
    return pl.pallas_call(
        kern,
        grid=(rows // tm,),
        in_specs=[
            pl.BlockSpec((tm, d), lambda i: (i, 0)),
            _const_spec((1, d)),
            _const_spec((d, 2 * width)),
            _const_spec((1, width)),
            _const_spec((1, width)),
            pl.BlockSpec((1, groups, CHUNK, CHUNK), lambda i: (i // npt, 0, 0, 0)),
            pl.BlockSpec((1, CHUNK, width), lambda i: (i // npt, 0, 0)),
            _const_spec((width, d)),
            _const_spec((1, d)),
        ],
        out_specs=[
            pl.BlockSpec((tm, d), lambda i: (i, 0)),
            pl.BlockSpec((1, CHUNK, width), lambda i: (jnp.minimum(i // tpb, prompt_batch - 1), 0, 0)),
            pl.BlockSpec((tm, width), lambda i: (jnp.maximum(i - npt, 0), 0)),
        ],
        out_shape=[
            jax.ShapeDtypeStruct((rows, d), F32),
            jax.ShapeDtypeStruct((prompt_batch, CHUNK, width), F32),
            jax.ShapeDtypeStruct((n_sample_rows, width), F32),
        ],
        scratch_shapes=[pltpu.VMEM((tm, width), BF16)],
        compiler_params=_params(),
        name="gmlp_mixer",
    )(x, gpre, win, lng, lnb, wmix, bmix, wout, gpost)


def _ffn_kernel(x_ref, gpre_ref, wg_ref, wu_ref, wd_ref, gpost_ref, xo_ref):
    x = x_ref[...]
    h = _rms(x, gpre_ref[...]).astype(BF16)
    g = jnp.dot(h, wg_ref[...], preferred_element_type=F32)
    u = jnp.dot(h, wu_ref[...], preferred_element_type=F32)
    a = (g * (1.0 / (1.0 + jnp.exp(-g))) * u).astype(BF16)
    out = jnp.dot(a, wd_ref[...], preferred_element_type=F32)
    xo_ref[...] = x + _rms(out, gpost_ref[...])


def _ffn(x, gpre, wg, wu, wd, gpost):
    rows, d = x.shape
    dff = wg.shape[1]
    tm = ROW_TILE
    return pl.pallas_call(
        _ffn_kernel,
        grid=(rows // tm,),
        in_specs=[
            pl.BlockSpec((tm, d), lambda i: (i, 0)),
            _const_spec((1, d)),
            _const_spec((d, dff)),
            _const_spec((d, dff)),
            _const_spec((dff, d)),
            _const_spec((1, d)),
        ],
        out_specs=pl.BlockSpec((tm, d), lambda i: (i, 0)),
        out_shape=jax.ShapeDtypeStruct((rows, d), F32),
        compiler_params=_params(),
        name="swiglu_ffn",
    )(x, gpre, wg, wu, wd, gpost)


def _kv_kernel(x_ref, g_ref, w_ref, b_ref, k_ref, v_ref):
    h = _rms(x_ref[...], g_ref[...]).astype(BF16)
    kv = jnp.dot(h, w_ref[...], preferred_element_type=F32) + b_ref[...]
    half = k_ref.shape[1]
    k_ref[...] = kv[:, :half]
    v_ref[...] = kv[:, half:]


def _kv(x, g, w, b):
    rows, d = x.shape
    half = w.shape[1] // 2
    tm = ROW_TILE
    return pl.pallas_call(
        _kv_kernel,
        grid=(rows // tm,),
        in_specs=[
            pl.BlockSpec((tm, d), lambda i: (i, 0)),
            _const_spec((1, d)),
            _const_spec((d, 2 * half)),
            _const_spec((1, 2 * half)),
        ],
        out_specs=[pl.BlockSpec((tm, half), lambda i: (i, 0))] * 2,
        out_shape=[jax.ShapeDtypeStruct((rows, half), F32)] * 2,
        compiler_params=_params(),
        name="shared_kv",
    )(x, g, w, b)


def _split_heads(k):
    kr = pltpu.roll(k, HEAD_DIM, k.ndim - 1)
    low = lax.broadcasted_iota(jnp.int32, k.shape, k.ndim - 1) < HEAD_DIM
    zero = jnp.zeros_like(k)
    head0 = (jnp.where(low, k, zero).astype(BF16), jnp.where(low, zero, kr).astype(BF16))
    head1 = (jnp.where(low, kr, zero).astype(BF16), jnp.where(low, zero, k).astype(BF16))
    return head0, head1


def _sink_column(sinks_ref, kh, parity, rows_per_pair, lead):
    pieces = []
    for j in range(PAIRS_PER_KV):
        s = sinks_ref[kh * HEADS_PER_KV + 2 * j + parity]
        pieces.append(jnp.full(lead + (rows_per_pair, 1), s, F32))
    return jnp.concatenate(pieces, axis=len(lead))


def _attn_kernel(sinks_ref, x_ref, gpre_ref, wq_ref, bq_ref, kown_ref, vown_ref, kprev_ref, vprev_ref,
                 ck_ref, cv_ref, wo_ref, bo_ref, gpost_ref, xo_ref, q_s, att_s,
                 *, n_prompt_tiles, tiles_per_batch, dec_seq):
    i = pl.program_id(0)
    tm, d = q_s.shape
    kv_lanes = PAIRS_PER_KV * LANES
    h = _rms(x_ref[...], gpre_ref[...]).astype(BF16)
    q = jnp.dot(h, wq_ref[...], preferred_element_type=F32) + bq_ref[...]
    q_s[...] = q * (HEAD_DIM ** -0.5)

    @pl.when(i < n_prompt_tiles)
    def _prompt():
        k_own = _split_heads(kown_ref[...])
        v_own = _split_heads(vown_ref[...])
        k_prev = _split_heads(kprev_ref[...])
        v_prev = _split_heads(vprev_ref[...])
        m_rows = PAIRS_PER_KV * CHUNK
        t = lax.broadcasted_iota(jnp.int32, (m_rows, 4 * CHUNK), 0) & (CHUNK - 1)
        c = lax.broadcasted_iota(jnp.int32, (m_rows, 4 * CHUNK), 1) & (2 * CHUNK - 1)
        in_prev = c < CHUNK
        valid = jnp.logical_and(c <= t + CHUNK, c > t)
        bias = jnp.where(valid, 0.0, NEG_INF)
        no_prev = jnp.where((i % tiles_per_batch) == 0, NEG_INF, 0.0)
        bias_first = jnp.where(in_prev, bias + no_prev, bias)
        lane_low = lax.broadcasted_iota(jnp.int32, (m_rows, LANES), 1) < HEAD_DIM
        for blk in range(tm // CHUNK):
            rows = slice(blk * CHUNK, (blk + 1) * CHUNK)
            prow = slice((blk - 1) * CHUNK, blk * CHUNK)
            mask = bias_first if blk == 0 else bias
            for kh in range(N_KV_HEADS):
                ka_o, kb_o = k_own[kh][0][rows], k_own[kh][1][rows]
                va_o, vb_o = v_own[kh][0][rows], v_own[kh][1][rows]
                if blk == 0:
                    ka_p, kb_p = k_prev[kh]
                    va_p, vb_p = v_prev[kh]
                else:
                    ka_p, kb_p = k_own[kh][0][prow], k_own[kh][1][prow]
                    va_p, vb_p = v_own[kh][0][prow], v_own[kh][1][prow]
                rhs_t = jnp.concatenate([ka_p, ka_o, kb_p, kb_o], axis=0)
                vv = jnp.concatenate([va_p, va_o, vb_p, vb_o], axis=0)
                qs = jnp.concatenate(
                    [q_s[rows, kh * kv_lanes + j * LANES: kh * kv_lanes + (j + 1) * LANES]
                     for j in range(PAIRS_PER_KV)], axis=0).astype(BF16)
                s = lax.dot_general(qs, rhs_t, (((1,), (1,)), ((), ())), preferred_element_type=F32)
                s = s + mask
                sa, sb = s[:, :2 * CHUNK], s[:, 2 * CHUNK:]
                sink_a = _sink_column(sinks_ref, kh, 0, CHUNK, ())
                sink_b = _sink_column(sinks_ref, kh, 1, CHUNK, ())
                ma = jnp.maximum(jnp.max(sa, axis=1, keepdims=True), sink_a)
                mb = jnp.maximum(jnp.max(sb, axis=1, keepdims=True), sink_b)
                pa = jnp.exp(sa - ma)
                pb = jnp.exp(sb - mb)
                den_a = jnp.sum(pa, axis=1, keepdims=True) + jnp.exp(sink_a - ma)
                den_b = jnp.sum(pb, axis=1, keepdims=True) + jnp.exp(sink_b - mb)
                p = jnp.concatenate([pa, pb], axis=1).astype(BF16)
                o = jnp.dot(p, vv, preferred_element_type=F32)
                o = o * jnp.where(lane_low, 1.0 / den_a, 1.0 / den_b)
                for j in range(PAIRS_PER_KV):
                    att_s[rows, kh * kv_lanes + j * LANES: kh * kv_lanes + (j + 1) * LANES] = (
                        o[j * CHUNK:(j + 1) * CHUNK].astype(BF16))

    @pl.when(i >= n_prompt_tiles)
    def _sample():
        nb = tm // dec_seq
        m_rows = PAIRS_PER_KV * dec_seq
        k_new = _split_heads(kown_ref[...].reshape(nb, dec_seq, LANES))
        v_new = _split_heads(vown_ref[...].reshape(nb, dec_seq, LANES))
        k_old = _split_heads(ck_ref[...].astype(F32))
        v_old = _split_heads(cv_ref[...].astype(F32))
        window = ck_ref.shape[1]
        t_c = lax.broadcasted_iota(jnp.int32, (nb, m_rows, 2 * window), 1) & (dec_seq - 1)
        c_c = lax.broadcasted_iota(jnp.int32, (nb, m_rows, 2 * window), 2) & (window - 1)
        valid_c = c_c > t_c
        t_n = lax.broadcasted_iota(jnp.int32, (nb, m_rows, 2 * dec_seq), 1) & (dec_seq - 1)
        col_n = lax.broadcasted_iota(jnp.int32, (nb, m_rows, 2 * dec_seq), 2)
        valid_n = (col_n & (dec_seq - 1)) <= t_n
        first_n = col_n < dec_seq
        lane_low = lax.broadcasted_iota(jnp.int32, (nb, m_rows, LANES), 2) < HEAD_DIM
        q3 = q_s[...].reshape(nb, dec_seq, d)
        bmm_nt = (((2,), (2,)), ((0,), (0,)))
        bmm_nn = (((2,), (1,)), ((0,), (0,)))
        for kh in range(N_KV_HEADS):
            rhs_c = jnp.concatenate([k_old[kh][0], k_old[kh][1]], axis=1)
            vv_c = jnp.concatenate([v_old[kh][0], v_old[kh][1]], axis=1)
            rhs_n = jnp.concatenate([k_new[kh][0].astype(F32), k_new[kh][1].astype(F32)], axis=1).astype(BF16)
            vv_n = jnp.concatenate([v_new[kh][0].astype(F32), v_new[kh][1].astype(F32)], axis=1).astype(BF16)
            qs = jnp.concatenate(
                [q3[:, :, kh * kv_lanes + j * LANES: kh * kv_lanes + (j + 1) * LANES]
                 for j in range(PAIRS_PER_KV)], axis=1).astype(BF16)
            s_c = lax.dot_general(qs, rhs_c, bmm_nt, preferred_element_type=F32)
            s_n = lax.dot_general(qs, rhs_n, bmm_nt, preferred_element_type=F32)
            s_c = jnp.where(valid_c, s_c, NEG_INF)
            s_n = jnp.where(valid_n, s_n, NEG_INF)
            sa, sb = s_c[:, :, :window], s_c[:, :, window:]
            sink_a = _sink_column(sinks_ref, kh, 0, dec_seq, (1,))
            sink_b = _sink_column(sinks_ref, kh, 1, dec_seq, (1,))
            na = jnp.max(jnp.where(first_n, s_n, NEG_INF), axis=2, keepdims=True)
            nb_ = jnp.max(jnp.where(first_n, NEG_INF, s_n), axis=2, keepdims=True)
            ma = jnp.maximum(jnp.maximum(jnp.max(sa, axis=2, keepdims=True), na), sink_a)
            mb = jnp.maximum(jnp.maximum(jnp.max(sb, axis=2, keepdims=True), nb_), sink_b)
            pa = jnp.exp(sa - ma)
            pb = jnp.exp(sb - mb)
            pn = jnp.exp(s_n - jnp.where(first_n, ma, mb))
            den_a = (jnp.sum(pa, axis=2, keepdims=True) + jnp.exp(sink_a - ma)
                     + jnp.sum(jnp.where(first_n, pn, 0.0), axis=2, keepdims=True))
            den_b = (jnp.sum(pb, axis=2, keepdims=True) + jnp.exp(sink_b - mb)
                     + jnp.sum(jnp.where(first_n, 0.0, pn), axis=2, keepdims=True))
            p_c = jnp.concatenate([pa, pb], axis=2).astype(BF16)
            o = lax.dot_general(p_c, vv_c, bmm_nn, preferred_element_type=F32)
            o = o + lax.dot_general(pn.astype(BF16), vv_n, bmm_nn, preferred_element_type=F32)
            o = o * jnp.where(lane_low, 1.0 / den_a, 1.0 / den_b)
            for j in range(PAIRS_PER_KV):
                att_s[:, kh * kv_lanes + j * LANES: kh * kv_lanes + (j + 1) * LANES] = (
                    o[:, j * dec_seq:(j + 1) * dec_seq, :].reshape(tm, LANES).astype(BF16))

    out = jnp.dot(att_s[...], wo_ref[...], preferred_element_type=F32) + bo_ref[...]
    xo_ref[...] = x_ref[...] + _rms(out, gpost_ref[...])


def _attn(x, k, v, ck, cv, sinks, gpre, wq, bq, wo, bo, gpost, *, n_prompt_rows, prompt_batch, dec_seq):
    rows, d = x.shape
    tm = ROW_TILE
    npt = n_prompt_rows // tm
    tpb = npt // prompt_batch
    blocks_per_tile = tm // CHUNK
    nb = tm // dec_seq
    window = ck.shape[1]
    kern = functools.partial(_attn_kernel, n_prompt_tiles=npt, tiles_per_batch=tpb, dec_seq=dec_seq)
    row_spec = pl.BlockSpec((tm, d), lambda i: (i, 0))
    own_spec = pl.BlockSpec((tm, LANES), lambda i: (i, 0))
    prev_spec = pl.BlockSpec((CHUNK, LANES), lambda i: (jnp.maximum(i * blocks_per_tile - 1, 0), 0))
    cache_spec = pl.BlockSpec((nb, window, LANES), lambda i: (jnp.maximum(i - npt, 0), 0, 0))
    return pl.pallas_call(
        kern,
        grid=(rows // tm,),
        in_specs=[
            pl.BlockSpec(memory_space=pltpu.SMEM),
            row_spec,
            _const_spec((1, d)),
            _const_spec((d, d)),
            _const_spec((1, d)),
            own_spec, own_spec, prev_spec, prev_spec,
            cache_spec, cache_spec,
            _const_spec((d, d)),
            _const_spec((1, d)),
            _const_spec((1, d)),
        ],
        out_specs=row_spec,
        out_shape=jax.ShapeDtypeStruct((rows, d), F32),
        scratch_shapes=[pltpu.VMEM((tm, d), F32), pltpu.VMEM((tm, d), BF16)],
        compiler_params=_params(),
        name="swa_attention",
    )(sinks, x, gpre, wq, bq, k, v, k, v, ck, cv, wo, bo, gpost)


def _mixing_weights(w_s, b_s, n):
    layers, groups = w_s.shape[:2]
    r = lax.broadcasted_iota(jnp.int32, (CHUNK, CHUNK), 0)
    c = lax.broadcasted_iota(jnp.int32, (CHUNK, CHUNK), 1)
    w_prompt = jnp.where(r >= c, w_s, 0.0)
    reps = CHUNK // n
    w_tiled = jnp.tile(w_s[:, :, :n, :n], (1, 1, reps, reps))
    w_sample = jnp.where(jnp.logical_and(r >= c, r // n == c // n), w_tiled, 0.0)
    wmix = jnp.stack([w_prompt, w_sample], axis=1).astype(BF16)
    b_prompt = jnp.repeat(jnp.swapaxes(b_s, 1, 2), LANES, axis=2)
    b_sample = jnp.tile(b_prompt[:, :n], (1, reps, 1))
    bmix = jnp.stack([b_prompt, b_sample], axis=1)
    return wmix.reshape(layers * 2, groups, CHUNK, CHUNK), bmix.reshape(layers * 2, CHUNK, -1)


def kernel(x_prompt, x_sample, cache_k, cache_v, sg_norm_pre, sg_w_in, sg_ln_g, sg_ln_b, sg_w_s, sg_b_s, sg_w_out, sg_norm_post, kv_norm, w_kv, b_kv, sw_norm_pre, sw_w_q, sw_b_q, sw_sinks, sw_w_o, sw_b_o, sw_norm_post, f_norm_pre, f_w_gate, f_w_up, f_w_down, f_norm_post):
    bp, sp, d = x_prompt.shape
    bs, ss, _ = x_sample.shape
    n_a = sg_w_in.shape[0]
    n_b = sw_w_q.shape[0]
    n_p, n_s = bp * sp, bs * ss
    x = jnp.concatenate([x_prompt.reshape(n_p, d), x_sample.reshape(n_s, d)], axis=0)

    wmix, bmix = _mixing_weights(sg_w_s, sg_b_s, ss)
    row = lambda a: a.reshape(1, -1)
    ck = cache_k.reshape(bs, cache_k.shape[1], -1).astype(BF16)
    cv = cache_v.reshape(bs, cache_v.shape[1], -1).astype(BF16)

    v_prompt_rows, v_sample_rows = [], []
    k_all = v_all = None
    for layer in range(n_a + n_b):
        if layer < n_a:
            l = layer
            x, vsp, vss = _mixer(
                x, row(sg_norm_pre[l]), sg_w_in[l].astype(BF16), row(sg_ln_g[l]), row(sg_ln_b[l]),
                wmix[2 * l:2 * l + 2], bmix[2 * l:2 * l + 2], sg_w_out[l].astype(BF16), row(sg_norm_post[l]),
                n_prompt_rows=n_p, prompt_batch=bp)
            v_prompt_rows.append(vsp)
            v_sample_rows.append(vss.reshape(bs, ss, -1))
        else:
            l = layer - n_a
            if l == 0:
                k_all, v_all = _kv(x, row(kv_norm), w_kv.astype(BF16), row(b_kv))
            x = _attn(x, k_all, v_all, ck, cv, sw_sinks[l], row(sw_norm_pre[l]), sw_w_q[l].astype(BF16),
                      row(sw_b_q[l]), sw_w_o[l].astype(BF16), row(sw_b_o[l]), row(sw_norm_post[l]),
                      n_prompt_rows=n_p, prompt_batch=bp, dec_seq=ss)
        x = _ffn(x, row(f_norm_pre[layer]), f_w_gate[layer].astype(BF16), f_w_up[layer].astype(BF16),
                 f_w_down[layer].astype(BF16), row(f_norm_post[layer]))

    y_prompt = x[:n_p].reshape(bp, sp, d)
    y_sample = x[n_p:].reshape(bs, ss, d)
    kvh = cache_k.shape[2]
    k_p = k_all[:n_p].reshape(bp, sp, kvh, -1)[:, sp - CHUNK:]
    v_p = v_all[:n_p].reshape(bp, sp, kvh, -1)[:, sp - CHUNK:]
    k_s = k_all[n_p:].reshape(bs, ss, kvh, -1)
    v_s = v_all[n_p:].reshape(bs, ss, kvh, -1)
    return (y_prompt, y_sample, jnp.stack(v_prompt_rows, axis=0), jnp.stack(v_sample_rows, axis=0),
            k_p, v_p, k_s, v_s)
```

```python
import functools

import jax
import jax.numpy as jnp
from jax import lax
from jax.experimental import pallas as pl
from jax.experimental.pallas import tpu as pltpu

F32 = jnp.float32
BF16 = jnp.bfloat16

RMS_EPS = 1e-6
LN_EPS = 1e-5
CHUNK = 128
LANES = 128
HEAD_DIM = 64
N_KV_HEADS = 2
HEADS_PER_KV = 8
PAIRS_PER_KV = HEADS_PER_KV // 2
ROW_TILE = 512
VMEM_LIMIT = 56 * 1024 * 1024
NEG_INF = float("-inf")


def _rms(x, g):
    ms = jnp.mean(x * x, axis=-1, keepdims=True)
    return x * lax.rsqrt(ms + RMS_EPS) * g


def _const_spec(shape):
    zeros = (0,) * len(shape)
    return pl.BlockSpec(shape, lambda i: zeros, pipeline_mode=pl.Buffered(1))


def _stream_specs(tm, width, npt):
    return [pl.BlockSpec((tm, width), lambda i: (jnp.minimum(i, npt - 1), 0)),
            pl.BlockSpec((tm, width), lambda i: (jnp.maximum(i - npt, 0), 0))]


def _stream_shapes(n_prompt_rows, n_sample_rows, width):
    return [jax.ShapeDtypeStruct((n_prompt_rows, width), F32),
            jax.ShapeDtypeStruct((n_sample_rows, width), F32)]


def _load_stream(i, npt, p_ref, s_ref):
    return lax.cond(i < npt, lambda: p_ref[...], lambda: s_ref[...])


def _store_stream(i, npt, p_ref, s_ref, value):
    @pl.when(i < npt)
    def _():
        p_ref[...] = value

    @pl.when(i >= npt)
    def _():
        s_ref[...] = value


def _params():
    return pltpu.CompilerParams(dimension_semantics=("arbitrary",), vmem_limit_bytes=VMEM_LIMIT)


def _mixer_kernel(xp_ref, xs_ref, gpre_ref, win_ref, lng_ref, lnb_ref, wmix_ref, bmix_ref, wout_ref, gpost_ref,
                  xop_ref, xos_ref, vsp_ref, vss_ref, y_s, *, n_prompt_tiles, tiles_per_batch):
    i = pl.program_id(0)
    tm, width = y_s.shape
    x = _load_stream(i, n_prompt_tiles, xp_ref, xs_ref)
    h = _rms(x, gpre_ref[...]).astype(BF16)
    z = jnp.dot(h, win_ref[...], preferred_element_type=F32)
    z = 0.5 * z * (1.0 + lax.erf(z * (0.5 ** 0.5)))
    u = z[:, :width]
    v = z[:, width:]
    mu = jnp.mean(v, axis=-1, keepdims=True)
    vc = v - mu
    var = jnp.mean(vc * vc, axis=-1, keepdims=True)
    v = vc * lax.rsqrt(var + LN_EPS) * lng_ref[...] + lnb_ref[...]

    @pl.when(jnp.logical_and(i < n_prompt_tiles, i % tiles_per_batch == tiles_per_batch - 1))
    def _():
        vsp_ref[0] = v[tm - CHUNK:, :]

    @pl.when(i >= n_prompt_tiles)
    def _():
        vss_ref[...] = v

    vb = v.astype(BF16)
    bias = bmix_ref[0]
    for c in range(tm // CHUNK):
        rows = slice(c * CHUNK, (c + 1) * CHUNK)
        for g in range(width // LANES):
            cols = slice(g * LANES, (g + 1) * LANES)
            mixed = jnp.dot(wmix_ref[0, g], vb[rows, cols], preferred_element_type=F32)
            y_s[rows, cols] = (u[rows, cols] * (mixed + bias[:, cols])).astype(BF16)
    out = jnp.dot(y_s[...], wout_ref[...], preferred_element_type=F32)
    _store_stream(i, n_prompt_tiles, xop_ref, xos_ref, x + _rms(out, gpost_ref[...]))


def _mixer(xp, xs, gpre, win, lng, lnb, wmix, bmix, wout, gpost, *, prompt_batch):
    n_p, d = xp.shape
    n_s = xs.shape[0]
    width = wout.shape[0]
    tm = ROW_TILE
    npt = n_p // tm
    tpb = npt // prompt_batch
    groups = wmix.shape[1]
    kern = functools.partial(_mixer_kernel, n_prompt_tiles=npt, tiles_per_batch=tpb)
    return pl.pallas_call(
        kern,
        grid=((n_p + n_s) // tm,),
        in_specs=_stream_specs(tm, d, npt) + [
            _const_spec((1, d)),
            _const_spec((d, 2 * width)),
            _const_spec((1, width)),
            _const_spec((1, width)),
            pl.BlockSpec((1, groups, CHUNK, CHUNK), lambda i: (i // npt, 0, 0, 0)),
            pl.BlockSpec((1, CHUNK, width), lambda i: (i // npt, 0, 0)),
            _const_spec((width, d)),
            _const_spec((1, d)),
        ],
        out_specs=_stream_specs(tm, d, npt) + [
            pl.BlockSpec((1, CHUNK, width), lambda i: (jnp.minimum(i // tpb, prompt_batch - 1), 0, 0)),
            pl.BlockSpec((tm, width), lambda i: (jnp.maximum(i - npt, 0), 0)),
        ],
        out_shape=_stream_shapes(n_p, n_s, d) + [
            jax.ShapeDtypeStruct((prompt_batch, CHUNK, width), F32),
            jax.ShapeDtypeStruct((n_s, width), F32),
        ],
        scratch_shapes=[pltpu.VMEM((tm, width), BF16)],
        compiler_params=_params(),
        name="gmlp_mixer",
    )(xp, xs, gpre, win, lng, lnb, wmix, bmix, wout, gpost)


def _ffn_kernel(*refs, n_prompt_tiles, with_kv):
    xp_ref, xs_ref, gpre_ref, wg_ref, wu_ref, wd_ref, gpost_ref = refs[:7]
    i = pl.program_id(0)
    x = _load_stream(i, n_prompt_tiles, xp_ref, xs_ref)
    h = _rms(x, gpre_ref[...]).astype(BF16)
    g = jnp.dot(h, wg_ref[...], preferred_element_type=F32)
    u = jnp.dot(h, wu_ref[...], preferred_element_type=F32)
    a = (g * (1.0 / (1.0 + jnp.exp(-g))) * u).astype(BF16)
    out = jnp.dot(a, wd_ref[...], preferred_element_type=F32)
    x_new = x + _rms(out, gpost_ref[...])
    if not with_kv:
        xop_ref, xos_ref = refs[7:]
        _store_stream(i, n_prompt_tiles, xop_ref, xos_ref, x_new)
        return
    kvg_ref, wkv_ref, bkv_ref, xop_ref, xos_ref, kp_ref, ks_ref, vp_ref, vs_ref = refs[7:]
    _store_stream(i, n_prompt_tiles, xop_ref, xos_ref, x_new)
    hk = _rms(x_new, kvg_ref[...]).astype(BF16)
    kv = jnp.dot(hk, wkv_ref[...], preferred_element_type=F32) + bkv_ref[...]
    half = kp_ref.shape[1]
    _store_stream(i, n_prompt_tiles, kp_ref, ks_ref, kv[:, :half])
    _store_stream(i, n_prompt_tiles, vp_ref, vs_ref, kv[:, half:])


def _ffn(xp, xs, gpre, wg, wu, wd, gpost, kv_weights=None):
    n_p, d = xp.shape
    n_s = xs.shape[0]
    dff = wg.shape[1]
    tm = ROW_TILE
    npt = n_p // tm
    with_kv = kv_weights is not None
    in_specs = _stream_specs(tm, d, npt) + [
        _const_spec((1, d)),
        _const_spec((d, dff)),
        _const_spec((d, dff)),
        _const_spec((dff, d)),
        _const_spec((1, d)),
    ]
    out_specs = _stream_specs(tm, d, npt)
    out_shape = _stream_shapes(n_p, n_s, d)
    args = [xp, xs, gpre, wg, wu, wd, gpost]
    if with_kv:
        kvg, wkv, bkv = kv_weights
        half = wkv.shape[1] // 2
        in_specs += [_const_spec((1, d)), _const_spec((d, 2 * half)), _const_spec((1, 2 * half))]
        out_specs += _stream_specs(tm, half, npt) * 2
        out_shape += _stream_shapes(n_p, n_s, half) * 2
        args += [kvg, wkv, bkv]
    return pl.pallas_call(
        functools.partial(_ffn_kernel, n_prompt_tiles=npt, with_kv=with_kv),
        grid=((n_p + n_s) // tm,),
        in_specs=in_specs,
        out_specs=out_specs,
        out_shape=out_shape,
        compiler_params=_params(),
        name="swiglu_ffn_kv" if with_kv else "swiglu_ffn",
    )(*args)


def _split_heads(k):
    kr = pltpu.roll(k, HEAD_DIM, k.ndim - 1)
    low = lax.broadcasted_iota(jnp.int32, k.shape, k.ndim - 1) < HEAD_DIM
    zero = jnp.zeros_like(k)
    head0 = (jnp.where(low, k, zero).astype(BF16), jnp.where(low, zero, kr).astype(BF16))
    head1 = (jnp.where(low, kr, zero).astype(BF16), jnp.where(low, zero, k).astype(BF16))
    return head0, head1


def _split_heads_t(v):
    vt = v.T
    vt_swapped = jnp.concatenate([vt[HEAD_DIM:], vt[:HEAD_DIM]], axis=0)
    top = lax.broadcasted_iota(jnp.int32, vt.shape, 0) < HEAD_DIM
    zero = jnp.zeros_like(vt)
    head0 = (jnp.where(top, vt, zero).astype(BF16), jnp.where(top, zero, vt_swapped).astype(BF16))
    head1 = (jnp.where(top, vt_swapped, zero).astype(BF16), jnp.where(top, zero, vt).astype(BF16))
    return head0, head1


def _sink_column(sinks_ref, kh, parity, rows_per_pair, lead):
    pieces = []
    for j in range(PAIRS_PER_KV):
        s = sinks_ref[kh * HEADS_PER_KV + 2 * j + parity]
        pieces.append(jnp.full(lead + (rows_per_pair, 1), s, F32))
    return jnp.concatenate(pieces, axis=len(lead))


def _sink_row(sinks_ref, kh, parity):
    pieces = [jnp.full((1, CHUNK), sinks_ref[kh * HEADS_PER_KV + 2 * j + parity], F32)
              for j in range(PAIRS_PER_KV)]
    return jnp.concatenate(pieces, axis=1)


def _prompt_attention(i, q_s, att_s, sinks_ref, k_own, k_prev, v_own, v_prev, tiles_per_batch):
    tm = q_s.shape[0]
    kv_lanes = PAIRS_PER_KV * LANES
    n_q = PAIRS_PER_KV * CHUNK
    k_own = _split_heads(k_own)
    k_prev = _split_heads(k_prev)
    v_own = _split_heads_t(v_own)
    v_prev = _split_heads_t(v_prev)
    slot = lax.broadcasted_iota(jnp.int32, (2 * CHUNK, n_q), 0) & (CHUNK - 1)
    t = lax.broadcasted_iota(jnp.int32, (2 * CHUNK, n_q), 1) & (CHUNK - 1)
    own = slot <= t
    no_prev = jnp.where((i % tiles_per_batch) == 0, NEG_INF, 0.0)
    top = lax.broadcasted_iota(jnp.int32, (LANES, n_q), 0) < HEAD_DIM
    for blk in range(tm // CHUNK):
        rows = slice(blk * CHUNK, (blk + 1) * CHUNK)
        prow = slice((blk - 1) * CHUNK, blk * CHUNK)
        for kh in range(N_KV_HEADS):
            ka_o, kb_o = k_own[kh][0][rows], k_own[kh][1][rows]
            va_o, vb_o = v_own[kh][0][:, rows], v_own[kh][1][:, rows]
            if blk == 0:
                ka_p, kb_p = k_prev[kh]
                va_p, vb_p = v_prev[kh]
            else:
                ka_p, kb_p = k_own[kh][0][prow], k_own[kh][1][prow]
                va_p, vb_p = v_own[kh][0][:, prow], v_own[kh][1][:, prow]
            keys = jnp.concatenate([ka_o, kb_o, ka_p, kb_p], axis=0)
            vals_t = jnp.concatenate([va_o, vb_o, va_p, vb_p], axis=1)
            qs = jnp.concatenate(
                [q_s[rows, kh * kv_lanes + j * LANES: kh * kv_lanes + (j + 1) * LANES]
                 for j in range(PAIRS_PER_KV)], axis=0).astype(BF16)
            s4 = lax.dot_general(keys, qs, (((1,), (1,)), ((), ())), preferred_element_type=F32)
            s_prev = s4[2 * CHUNK:]
            if blk == 0:
                s_prev = s_prev + no_prev
            s = jnp.where(own, s4[:2 * CHUNK], s_prev)
            sa, sb = s[:CHUNK], s[CHUNK:]
            sink_a = _sink_row(sinks_ref, kh, 0)
            sink_b = _sink_row(sinks_ref, kh, 1)
            ma = jnp.maximum(jnp.max(sa, axis=0, keepdims=True), sink_a)
            mb = jnp.maximum(jnp.max(sb, axis=0, keepdims=True), sink_b)
            pa = jnp.exp(sa - ma)
            pb = jnp.exp(sb - mb)
            inv_a = 1.0 / (jnp.sum(pa, axis=0, keepdims=True) + jnp.exp(sink_a - ma))
            inv_b = 1.0 / (jnp.sum(pb, axis=0, keepdims=True) + jnp.exp(sink_b - mb))
            p = jnp.concatenate([pa, pb], axis=0)
            p4 = jnp.concatenate([jnp.where(own, p, 0.0), jnp.where(own, 0.0, p)], axis=0).astype(BF16)
            o_t = jnp.dot(vals_t, p4, preferred_element_type=F32)
            o_t = o_t * jnp.where(top, inv_a, inv_b)
            for j in range(PAIRS_PER_KV):
                att_s[rows, kh * kv_lanes + j * LANES: kh * kv_lanes + (j + 1) * LANES] = (
                    o_t[:, j * CHUNK:(j + 1) * CHUNK].T.astype(BF16))


def _sample_attention(q_s, att_s, sinks_ref, k_new, v_new, k_old, v_old, dec_seq):
    tm, d = q_s.shape
    kv_lanes = PAIRS_PER_KV * LANES
    nb = tm // dec_seq
    m_rows = PAIRS_PER_KV * dec_seq
    window = k_old.shape[1]
    k_new = _split_heads(k_new.reshape(nb, dec_seq, LANES))
    v_new = _split_heads(v_new.reshape(nb, dec_seq, LANES))
    k_old = _split_heads(k_old.astype(F32))
    v_old = _split_heads(v_old.astype(F32))
    t_c = lax.broadcasted_iota(jnp.int32, (nb, m_rows, 2 * window), 1) & (dec_seq - 1)
    c_c = lax.broadcasted_iota(jnp.int32, (nb, m_rows, 2 * window), 2) & (window - 1)
    valid_c = c_c > t_c
    t_n = lax.broadcasted_iota(jnp.int32, (nb, m_rows, 2 * dec_seq), 1) & (dec_seq - 1)
    col_n = lax.broadcasted_iota(jnp.int32, (nb, m_rows, 2 * dec_seq), 2)
    valid_n = (col_n & (dec_seq - 1)) <= t_n
    first_n = col_n < dec_seq
    lane_low = lax.broadcasted_iota(jnp.int32, (nb, m_rows, LANES), 2) < HEAD_DIM
    q3 = q_s[...].reshape(nb, dec_seq, d)
    bmm_nt = (((2,), (2,)), ((0,), (0,)))
    bmm_nn = (((2,), (1,)), ((0,), (0,)))
    for kh in range(N_KV_HEADS):
        rhs_c = jnp.concatenate([k_old[kh][0], k_old[kh][1]], axis=1)
        vv_c = jnp.concatenate([v_old[kh][0], v_old[kh][1]], axis=1)
        rhs_n = jnp.concatenate([k_new[kh][0].astype(F32), k_new[kh][1].astype(F32)], axis=1).astype(BF16)
        vv_n = jnp.concatenate([v_new[kh][0].astype(F32), v_new[kh][1].astype(F32)], axis=1).astype(BF16)
        qs = jnp.concatenate(
            [q3[:, :, kh * kv_lanes + j * LANES: kh * kv_lanes + (j + 1) * LANES]
             for j in range(PAIRS_PER_KV)], axis=1).astype(BF16)
        s_c = lax.dot_general(qs, rhs_c, bmm_nt, preferred_element_type=F32)
        s_n = lax.dot_general(qs, rhs_n, bmm_nt, preferred_element_type=F32)
        s_c = jnp.where(valid_c, s_c, NEG_INF)
        s_n = jnp.where(valid_n, s_n, NEG_INF)
        sa, sb = s_c[:, :, :window], s_c[:, :, window:]
        sink_a = _sink_column(sinks_ref, kh, 0, dec_seq, (1,))
        sink_b = _sink_column(sinks_ref, kh, 1, dec_seq, (1,))
        na = jnp.max(jnp.where(first_n, s_n, NEG_INF), axis=2, keepdims=True)
        nb_ = jnp.max(jnp.where(first_n, NEG_INF, s_n), axis=2, keepdims=True)
        ma = jnp.maximum(jnp.maximum(jnp.max(sa, axis=2, keepdims=True), na), sink_a)
        mb = jnp.maximum(jnp.maximum(jnp.max(sb, axis=2, keepdims=True), nb_), sink_b)
        pa = jnp.exp(sa - ma)
        pb = jnp.exp(sb - mb)
        pn = jnp.exp(s_n - jnp.where(first_n, ma, mb))
        den_a = (jnp.sum(pa, axis=2, keepdims=True) + jnp.exp(sink_a - ma)
                 + jnp.sum(jnp.where(first_n, pn, 0.0), axis=2, keepdims=True))
        den_b = (jnp.sum(pb, axis=2, keepdims=True) + jnp.exp(sink_b - mb)
                 + jnp.sum(jnp.where(first_n, 0.0, pn), axis=2, keepdims=True))
        p_c = jnp.concatenate([pa, pb], axis=2).astype(BF16)
        o = lax.dot_general(p_c, vv_c, bmm_nn, preferred_element_type=F32)
        o = o + lax.dot_general(pn.astype(BF16), vv_n, bmm_nn, preferred_element_type=F32)
        o = o * jnp.where(lane_low, 1.0 / den_a, 1.0 / den_b)
        for j in range(PAIRS_PER_KV):
            att_s[:, kh * kv_lanes + j * LANES: kh * kv_lanes + (j + 1) * LANES] = (
                o[:, j * dec_seq:(j + 1) * dec_seq, :].reshape(tm, LANES).astype(BF16))


def _attn_kernel(sinks_ref, xp_ref, xs_ref, gpre_ref, wq_ref, bq_ref, kp_ref, ks_ref, vp_ref, vs_ref,
                 kprev_ref, vprev_ref, ck_ref, cv_ref, wo_ref, bo_ref, gpost_ref, xop_ref, xos_ref,
                 q_s, att_s, *, n_prompt_tiles, tiles_per_batch, dec_seq):
    i = pl.program_id(0)
    x = _load_stream(i, n_prompt_tiles, xp_ref, xs_ref)
    h = _rms(x, gpre_ref[...]).astype(BF16)
    q = jnp.dot(h, wq_ref[...], preferred_element_type=F32) + bq_ref[...]
    q_s[...] = q * (HEAD_DIM ** -0.5)

    @pl.when(i < n_prompt_tiles)
    def _():
        _prompt_attention(i, q_s, att_s, sinks_ref, kp_ref[...], kprev_ref[...], vp_ref[...], vprev_ref[...],
                          tiles_per_batch)

    @pl.when(i >= n_prompt_tiles)
    def _():
        _sample_attention(q_s, att_s, sinks_ref, ks_ref[...], vs_ref[...], ck_ref[...], cv_ref[...], dec_seq)

    out = jnp.dot(att_s[...], wo_ref[...], preferred_element_type=F32) + bo_ref[...]
    x = _load_stream(i, n_prompt_tiles, xp_ref, xs_ref)
    _store_stream(i, n_prompt_tiles, xop_ref, xos_ref, x + _rms(out, gpost_ref[...]))


def _attn(xp, xs, kp, ks, vp, vs, ck, cv, sinks, gpre, wq, bq, wo, bo, gpost, *, prompt_batch, dec_seq):
    n_p, d = xp.shape
    n_s = xs.shape[0]
    tm = ROW_TILE
    npt = n_p // tm
    tpb = npt // prompt_batch
    blocks_per_tile = tm // CHUNK
    nb = tm // dec_seq
    window = ck.shape[1]
    kern = functools.partial(_attn_kernel, n_prompt_tiles=npt, tiles_per_batch=tpb, dec_seq=dec_seq)
    last_prompt_block = n_p // CHUNK - 1
    prev_spec = pl.BlockSpec(
        (CHUNK, LANES), lambda i: (jnp.clip(i * blocks_per_tile - 1, 0, last_prompt_block), 0))
    cache_spec = pl.BlockSpec((nb, window, LANES), lambda i: (jnp.maximum(i - npt, 0), 0, 0))
    return pl.pallas_call(
        kern,
        grid=((n_p + n_s) // tm,),
        in_specs=[pl.BlockSpec(memory_space=pltpu.SMEM)] + _stream_specs(tm, d, npt) + [
            _const_spec((1, d)),
            _const_spec((d, d)),
            _const_spec((1, d)),
        ] + _stream_specs(tm, LANES, npt) * 2 + [
            prev_spec, prev_spec,
            cache_spec, cache_spec,
            _const_spec((d, d)),
            _const_spec((1, d)),
            _const_spec((1, d)),
        ],
        out_specs=_stream_specs(tm, d, npt),
        out_shape=_stream_shapes(n_p, n_s, d),
        scratch_shapes=[pltpu.VMEM((tm, d), F32), pltpu.VMEM((tm, d), BF16)],
        compiler_params=_params(),
        name="swa_attention",
    )(sinks, xp, xs, gpre, wq, bq, kp, ks, vp, vs, kp, vp, ck, cv, wo, bo, gpost)


def _mixing_weights(w_s, b_s, n):
    layers, groups = w_s.shape[:2]
    r = lax.broadcasted_iota(jnp.int32, (CHUNK, CHUNK), 0)
    c = lax.broadcasted_iota(jnp.int32, (CHUNK, CHUNK), 1)
    w_prompt = jnp.where(r >= c, w_s, 0.0)
    reps = CHUNK // n
    w_tiled = jnp.tile(w_s[:, :, :n, :n], (1, 1, reps, reps))
    w_sample = jnp.where(jnp.logical_and(r >= c, r // n == c // n), w_tiled, 0.0)
    wmix = jnp.stack([w_prompt, w_sample], axis=1).astype(BF16)
    b_prompt = jnp.repeat(jnp.swapaxes(b_s, 1, 2), LANES, axis=2)
    b_sample = jnp.tile(b_prompt[:, :n], (1, reps, 1))
    bmix = jnp.stack([b_prompt, b_sample], axis=1)
    return wmix.reshape(layers * 2, groups, CHUNK, CHUNK), bmix.reshape(layers * 2, CHUNK, -1)


def kernel(x_prompt, x_sample, cache_k, cache_v, sg_norm_pre, sg_w_in, sg_ln_g, sg_ln_b, sg_w_s, sg_b_s, sg_w_out, sg_norm_post, kv_norm, w_kv, b_kv, sw_norm_pre, sw_w_q, sw_b_q, sw_sinks, sw_w_o, sw_b_o, sw_norm_post, f_norm_pre, f_w_gate, f_w_up, f_w_down, f_norm_post):
    bp, sp, d = x_prompt.shape
    bs, ss, _ = x_sample.shape
    n_a = sg_w_in.shape[0]
    n_b = sw_w_q.shape[0]
    xp = x_prompt.reshape(bp * sp, d)
    xs = x_sample.reshape(bs * ss, d)

    wmix, bmix = _mixing_weights(sg_w_s, sg_b_s, ss)
    row = lambda a: a.reshape(1, -1)
    kvh = cache_k.shape[2]
    ck = cache_k.reshape(bs, cache_k.shape[1], -1).astype(BF16)
    cv = cache_v.reshape(bs, cache_v.shape[1], -1).astype(BF16)

    v_prompt_rows, v_sample_rows = [], []
    kp = ks = vp = vs = None
    for layer in range(n_a + n_b):
        if layer < n_a:
            l = layer
            xp, xs, vsp, vss = _mixer(
                xp, xs, row(sg_norm_pre[l]), sg_w_in[l].astype(BF16), row(sg_ln_g[l]), row(sg_ln_b[l]),
                wmix[2 * l:2 * l + 2], bmix[2 * l:2 * l + 2], sg_w_out[l].astype(BF16), row(sg_norm_post[l]),
                prompt_batch=bp)
            v_prompt_rows.append(vsp)
            v_sample_rows.append(vss.reshape(bs, ss, -1))
        else:
            l = layer - n_a
            xp, xs = _attn(xp, xs, kp, ks, vp, vs, ck, cv, sw_sinks[l], row(sw_norm_pre[l]),
                           sw_w_q[l].astype(BF16), row(sw_b_q[l]), sw_w_o[l].astype(BF16), row(sw_b_o[l]),
                           row(sw_norm_post[l]), prompt_batch=bp, dec_seq=ss)
        ffn_weights = (row(f_norm_pre[layer]), f_w_gate[layer].astype(BF16), f_w_up[layer].astype(BF16),
                       f_w_down[layer].astype(BF16), row(f_norm_post[layer]))
        if layer == n_a - 1:
            xp, xs, kp, ks, vp, vs = _ffn(xp, xs, *ffn_weights,
                                          kv_weights=(row(kv_norm), w_kv.astype(BF16), row(b_kv)))
        else:
            xp, xs = _ffn(xp, xs, *ffn_weights)

    def last_window(a):
        return a.reshape(bp, sp, -1)[:, sp - CHUNK:].reshape(bp, CHUNK, kvh, -1)

    return (xp.reshape(bp, sp, d), xs.reshape(bs, ss, d),
            jnp.stack(v_prompt_rows, axis=0), jnp.stack(v_sample_rows, axis=0),
            last_window(kp), last_window(vp), ks.reshape(bs, ss, kvh, -1), vs.reshape(bs, ss, kvh, -1))
```

```python
import functools

import jax
import jax.numpy as jnp
from jax import lax
from jax.experimental import pallas as pl
from jax.experimental.pallas import tpu as pltpu

F32 = jnp.float32
BF16 = jnp.bfloat16

RMS_EPS = 1e-6
LN_EPS = 1e-5
CHUNK = 128
LANES = 128
HEAD_DIM = 64
N_KV_HEADS = 2
HEADS_PER_KV = 8
PAIRS_PER_KV = HEADS_PER_KV // 2
ROW_TILE = 512
VMEM_LIMIT = 56 * 1024 * 1024
NEG_INF = float("-inf")


def _rms(x, g):
    ms = jnp.mean(x * x, axis=-1, keepdims=True)
    return x * lax.rsqrt(ms + RMS_EPS) * g


def _const_spec(shape):
    zeros = (0,) * len(shape)
    return pl.BlockSpec(shape, lambda i: zeros, pipeline_mode=pl.Buffered(1))


def _row_spec(tm, width, first_tile=0):
    return pl.BlockSpec((tm, width), lambda i: (i + first_tile, 0))


def _params():
    return pltpu.CompilerParams(dimension_semantics=("arbitrary",), vmem_limit_bytes=VMEM_LIMIT)


def _mixer_kernel(*refs, n_prompt_tiles, tiles_per_batch, split_input):
    n_x = 2 if split_input else 1
    gpre_ref, win_ref, lng_ref, lnb_ref, wmix_ref, bmix_ref, wout_ref, gpost_ref = refs[n_x:n_x + 8]
    xo_ref, vsp_ref, vss_ref, y_s = refs[n_x + 8:]
    i = pl.program_id(0)
    tm, width = y_s.shape
    if split_input:
        x = jnp.where(i < n_prompt_tiles, refs[0][...], refs[1][...])
    else:
        x = refs[0][...]
    h = _rms(x, gpre_ref[...]).astype(BF16)
    z = jnp.dot(h, win_ref[...], preferred_element_type=F32)
    z = 0.5 * z * (1.0 + lax.erf(z * (0.5 ** 0.5)))
    u = z[:, :width]
    v = z[:, width:]
    mu = jnp.mean(v, axis=-1, keepdims=True)
    vc = v - mu
    var = jnp.mean(vc * vc, axis=-1, keepdims=True)
    v = vc * lax.rsqrt(var + LN_EPS) * lng_ref[...] + lnb_ref[...]

    @pl.when(jnp.logical_and(i < n_prompt_tiles, i % tiles_per_batch == tiles_per_batch - 1))
    def _():
        vsp_ref[0] = v[tm - CHUNK:, :]

    @pl.when(i >= n_prompt_tiles)
    def _():
        vss_ref[...] = v

    vb = v.astype(BF16)
    bias = bmix_ref[0]
    for c in range(tm // CHUNK):
        rows = slice(c * CHUNK, (c + 1) * CHUNK)
        for g in range(width // LANES):
            cols = slice(g * LANES, (g + 1) * LANES)
            mixed = jnp.dot(wmix_ref[0, g], vb[rows, cols], preferred_element_type=F32)
            y_s[rows, cols] = (u[rows, cols] * (mixed + bias[:, cols])).astype(BF16)
    out = jnp.dot(y_s[...], wout_ref[...], preferred_element_type=F32)
    xo_ref[...] = x + _rms(out, gpost_ref[...])


def _mixer(xs, gpre, win, lng, lnb, wmix, bmix, wout, gpost, *, n_prompt_rows, n_sample_rows, prompt_batch):
    d = xs[0].shape[1]
    width = wout.shape[0]
    tm = ROW_TILE
    npt = n_prompt_rows // tm
    tpb = npt // prompt_batch
    groups = wmix.shape[1]
    split_input = len(xs) == 2
    if split_input:
        x_specs = [pl.BlockSpec((tm, d), lambda i: (jnp.minimum(i, npt - 1), 0)),
                   pl.BlockSpec((tm, d), lambda i: (jnp.maximum(i - npt, 0), 0))]
    else:
        x_specs = [_row_spec(tm, d)]
    kern = functools.partial(_mixer_kernel, n_prompt_tiles=npt, tiles_per_batch=tpb, split_input=split_input)
    return pl.pallas_call(
        kern,
        grid=((n_prompt_rows + n_sample_rows) // tm,),
        in_specs=x_specs + [
            _const_spec((1, d)),
            _const_spec((d, 2 * width)),
            _const_spec((1, width)),
            _const_spec((1, width)),
            pl.BlockSpec((1, groups, CHUNK, CHUNK), lambda i: (i // npt, 0, 0, 0)),
            pl.BlockSpec((1, CHUNK, width), lambda i: (i // npt, 0, 0)),
            _const_spec((width, d)),
            _const_spec((1, d)),
        ],
        out_specs=[
            _row_spec(tm, d),
            pl.BlockSpec((1, CHUNK, width), lambda i: (jnp.minimum(i // tpb, prompt_batch - 1), 0, 0)),
            pl.BlockSpec((tm, width), lambda i: (jnp.maximum(i - npt, 0), 0)),
        ],
        out_shape=[
            jax.ShapeDtypeStruct((n_prompt_rows + n_sample_rows, d), F32),
            jax.ShapeDtypeStruct((prompt_batch, CHUNK, width), F32),
            jax.ShapeDtypeStruct((n_sample_rows, width), F32),
        ],
        scratch_shapes=[pltpu.VMEM((tm, width), BF16)],
        compiler_params=_params(),
        name="gmlp_mixer",
    )(*xs, gpre, win, lng, lnb, wmix, bmix, wout, gpost)


def _ffn_kernel(*refs, with_kv):
    x_ref, gpre_ref, wg_ref, wu_ref, wd_ref, gpost_ref = refs[:6]
    x = x_ref[...]
    h = _rms(x, gpre_ref[...]).astype(BF16)
    g = jnp.dot(h, wg_ref[...], preferred_element_type=F32)
    u = jnp.dot(h, wu_ref[...], preferred_element_type=F32)
    a = (g * (1.0 / (1.0 + jnp.exp(-g))) * u).astype(BF16)
    out = jnp.dot(a, wd_ref[...], preferred_element_type=F32)
    x_new = x + _rms(out, gpost_ref[...])
    if not with_kv:
        refs[6][...] = x_new
        return
    kvg_ref, wkv_ref, bkv_ref, xo_ref, k_ref, v_ref = refs[6:]
    xo_ref[...] = x_new
    hk = _rms(x_new, kvg_ref[...]).astype(BF16)
    kv = jnp.dot(hk, wkv_ref[...], preferred_element_type=F32) + bkv_ref[...]
    half = k_ref.shape[1]
    k_ref[...] = kv[:, :half]
    v_ref[...] = kv[:, half:]


def _ffn(x, gpre, wg, wu, wd, gpost, kv_weights=None, first_row=0, n_rows=None):
    d = x.shape[1]
    dff = wg.shape[1]
    tm = ROW_TILE
    n_rows = x.shape[0] if n_rows is None else n_rows
    with_kv = kv_weights is not None
    in_specs = [
        _row_spec(tm, d, first_row // tm),
        _const_spec((1, d)),
        _const_spec((d, dff)),
        _const_spec((d, dff)),
        _const_spec((dff, d)),
        _const_spec((1, d)),
    ]
    out_specs = [_row_spec(tm, d)]
    out_shape = [jax.ShapeDtypeStruct((n_rows, d), F32)]
    args = [x, gpre, wg, wu, wd, gpost]
    if with_kv:
        kvg, wkv, bkv = kv_weights
        half = wkv.shape[1] // 2
        in_specs += [_const_spec((1, d)), _const_spec((d, 2 * half)), _const_spec((1, 2 * half))]
        out_specs += [_row_spec(tm, half)] * 2
        out_shape += [jax.ShapeDtypeStruct((n_rows, half), F32)] * 2
        args += [kvg, wkv, bkv]
    out = pl.pallas_call(
        functools.partial(_ffn_kernel, with_kv=with_kv),
        grid=(n_rows // tm,),
        in_specs=in_specs,
        out_specs=out_specs,
        out_shape=out_shape,
        compiler_params=_params(),
        name="swiglu_ffn_kv" if with_kv else "swiglu_ffn",
    )(*args)
    return out if with_kv else out[0]


def _split_heads(k):
    kr = pltpu.roll(k, HEAD_DIM, k.ndim - 1)
    low = lax.broadcasted_iota(jnp.int32, k.shape, k.ndim - 1) < HEAD_DIM
    zero = jnp.zeros_like(k)
    head0 = (jnp.where(low, k, zero).astype(BF16), jnp.where(low, zero, kr).astype(BF16))
    head1 = (jnp.where(low, kr, zero).astype(BF16), jnp.where(low, zero, k).astype(BF16))
    return head0, head1


def _split_heads_t(v):
    vt = v.T
    vt_swapped = jnp.concatenate([vt[HEAD_DIM:], vt[:HEAD_DIM]], axis=0)
    top = lax.broadcasted_iota(jnp.int32, vt.shape, 0) < HEAD_DIM
    zero = jnp.zeros_like(vt)
    head0 = (jnp.where(top, vt, zero).astype(BF16), jnp.where(top, zero, vt_swapped).astype(BF16))
    head1 = (jnp.where(top, vt_swapped, zero).astype(BF16), jnp.where(top, zero, vt).astype(BF16))
    return head0, head1


def _sink_column(sinks_ref, kh, parity, rows_per_pair, lead):
    pieces = []
    for j in range(PAIRS_PER_KV):
        s = sinks_ref[kh * HEADS_PER_KV + 2 * j + parity]
        pieces.append(jnp.full(lead + (rows_per_pair, 1), s, F32))
    return jnp.concatenate(pieces, axis=len(lead))


def _sink_row(sinks_ref, kh, parity):
    pieces = [jnp.full((1, CHUNK), sinks_ref[kh * HEADS_PER_KV + 2 * j + parity], F32)
              for j in range(PAIRS_PER_KV)]
    return jnp.concatenate(pieces, axis=1)


def _prompt_attention(i, q_s, att_s, sinks_ref, k_own, k_prev, v_own, v_prev, tiles_per_batch):
    tm = q_s.shape[0]
    kv_lanes = PAIRS_PER_KV * LANES
    n_q = PAIRS_PER_KV * CHUNK
    k_own = _split_heads(k_own)
    k_prev = _split_heads(k_prev)
    v_own = _split_heads_t(v_own)
    v_prev = _split_heads_t(v_prev)
    slot = lax.broadcasted_iota(jnp.int32, (2 * CHUNK, n_q), 0) & (CHUNK - 1)
    t = lax.broadcasted_iota(jnp.int32, (2 * CHUNK, n_q), 1) & (CHUNK - 1)
    own = slot <= t
    no_prev = jnp.where((i % tiles_per_batch) == 0, NEG_INF, 0.0)
    top = lax.broadcasted_iota(jnp.int32, (LANES, n_q), 0) < HEAD_DIM

    def scores(blk, kh):
        rows = slice(blk * CHUNK, (blk + 1) * CHUNK)
        prow = slice((blk - 1) * CHUNK, blk * CHUNK)
        ka_o, kb_o = k_own[kh][0][rows], k_own[kh][1][rows]
        ka_p, kb_p = k_prev[kh] if blk == 0 else (k_own[kh][0][prow], k_own[kh][1][prow])
        keys = jnp.concatenate([ka_o, kb_o, ka_p, kb_p], axis=0)
        qs = jnp.concatenate(
            [q_s[rows, kh * kv_lanes + j * LANES: kh * kv_lanes + (j + 1) * LANES]
             for j in range(PAIRS_PER_KV)], axis=0).astype(BF16)
        s4 = lax.dot_general(keys, qs, (((1,), (1,)), ((), ())), preferred_element_type=F32)
        s_prev = s4[2 * CHUNK:]
        if blk == 0:
            s_prev = s_prev + no_prev
        return jnp.where(own, s4[:2 * CHUNK], s_prev)

    def finish(blk, kh, s):
        rows = slice(blk * CHUNK, (blk + 1) * CHUNK)
        prow = slice((blk - 1) * CHUNK, blk * CHUNK)
        va_o, vb_o = v_own[kh][0][:, rows], v_own[kh][1][:, rows]
        va_p, vb_p = v_prev[kh] if blk == 0 else (v_own[kh][0][:, prow], v_own[kh][1][:, prow])
        vals_t = jnp.concatenate([va_o, vb_o, va_p, vb_p], axis=1)
        sa, sb = s[:CHUNK], s[CHUNK:]
        sink_a = _sink_row(sinks_ref, kh, 0)
        sink_b = _sink_row(sinks_ref, kh, 1)
        ma = jnp.maximum(jnp.max(sa, axis=0, keepdims=True), sink_a)
        mb = jnp.maximum(jnp.max(sb, axis=0, keepdims=True), sink_b)
        pa = jnp.exp(sa - ma)
        pb = jnp.exp(sb - mb)
        inv_a = 1.0 / (jnp.sum(pa, axis=0, keepdims=True) + jnp.exp(sink_a - ma))
        inv_b = 1.0 / (jnp.sum(pb, axis=0, keepdims=True) + jnp.exp(sink_b - mb))
        p = jnp.concatenate([pa, pb], axis=0)
        p4 = jnp.concatenate([jnp.where(own, p, 0.0), jnp.where(own, 0.0, p)], axis=0).astype(BF16)
        o_t = jnp.dot(vals_t, p4, preferred_element_type=F32)
        o_t = o_t * jnp.where(top, inv_a, inv_b)
        for j in range(PAIRS_PER_KV):
            att_s[rows, kh * kv_lanes + j * LANES: kh * kv_lanes + (j + 1) * LANES] = (
                o_t[:, j * CHUNK:(j + 1) * CHUNK].T.astype(BF16))

    items = [(blk, kh) for blk in range(tm // CHUNK) for kh in range(N_KV_HEADS)]
    s_next = scores(*items[0])
    for n, item in enumerate(items):
        s_cur = s_next
        if n + 1 < len(items):
            s_next = scores(*items[n + 1])
        finish(*item, s_cur)


def _sample_attention(q_s, att_s, sinks_ref, k_new, v_new, k_old, v_old, dec_seq):
    tm, d = q_s.shape
    kv_lanes = PAIRS_PER_KV * LANES
    nb = tm // dec_seq
    m_rows = PAIRS_PER_KV * dec_seq
    window = k_old.shape[1]
    k_new = _split_heads(k_new.reshape(nb, dec_seq, LANES))
    v_new = _split_heads(v_new.reshape(nb, dec_seq, LANES))
    k_old = _split_heads(k_old.astype(F32))
    v_old = _split_heads(v_old.astype(F32))
    t_c = lax.broadcasted_iota(jnp.int32, (nb, m_rows, 2 * window), 1) & (dec_seq - 1)
    c_c = lax.broadcasted_iota(jnp.int32, (nb, m_rows, 2 * window), 2) & (window - 1)
    valid_c = c_c > t_c
    t_n = lax.broadcasted_iota(jnp.int32, (nb, m_rows, 2 * dec_seq), 1) & (dec_seq - 1)
    col_n = lax.broadcasted_iota(jnp.int32, (nb, m_rows, 2 * dec_seq), 2)
    valid_n = (col_n & (dec_seq - 1)) <= t_n
    first_n = col_n < dec_seq
    lane_low = lax.broadcasted_iota(jnp.int32, (nb, m_rows, LANES), 2) < HEAD_DIM
    q3 = q_s[...].reshape(nb, dec_seq, d)
    bmm_nt = (((2,), (2,)), ((0,), (0,)))
    bmm_nn = (((2,), (1,)), ((0,), (0,)))
    for kh in range(N_KV_HEADS):
        rhs_c = jnp.concatenate([k_old[kh][0], k_old[kh][1]], axis=1)
        vv_c = jnp.concatenate([v_old[kh][0], v_old[kh][1]], axis=1)
        rhs_n = jnp.concatenate([k_new[kh][0].astype(F32), k_new[kh][1].astype(F32)], axis=1).astype(BF16)
        vv_n = jnp.concatenate([v_new[kh][0].astype(F32), v_new[kh][1].astype(F32)], axis=1).astype(BF16)
        qs = jnp.concatenate(
            [q3[:, :, kh * kv_lanes + j * LANES: kh * kv_lanes + (j + 1) * LANES]
             for j in range(PAIRS_PER_KV)], axis=1).astype(BF16)
        s_c = lax.dot_general(qs, rhs_c, bmm_nt, preferred_element_type=F32)
        s_n = lax.dot_general(qs, rhs_n, bmm_nt, preferred_element_type=F32)
        s_c = jnp.where(valid_c, s_c, NEG_INF)
        s_n = jnp.where(valid_n, s_n, NEG_INF)
        sa, sb = s_c[:, :, :window], s_c[:, :, window:]
        sink_a = _sink_column(sinks_ref, kh, 0, dec_seq, (1,))
        sink_b = _sink_column(sinks_ref, kh, 1, dec_seq, (1,))
        na = jnp.max(jnp.where(first_n, s_n, NEG_INF), axis=2, keepdims=True)
        nb_ = jnp.max(jnp.where(first_n, NEG_INF, s_n), axis=2, keepdims=True)
        ma = jnp.maximum(jnp.maximum(jnp.max(sa, axis=2, keepdims=True), na), sink_a)
        mb = jnp.maximum(jnp.maximum(jnp.max(sb, axis=2, keepdims=True), nb_), sink_b)
        pa = jnp.exp(sa - ma)
        pb = jnp.exp(sb - mb)
        pn = jnp.exp(s_n - jnp.where(first_n, ma, mb))
        den_a = (jnp.sum(pa, axis=2, keepdims=True) + jnp.exp(sink_a - ma)
                 + jnp.sum(jnp.where(first_n, pn, 0.0), axis=2, keepdims=True))
        den_b = (jnp.sum(pb, axis=2, keepdims=True) + jnp.exp(sink_b - mb)
                 + jnp.sum(jnp.where(first_n, 0.0, pn), axis=2, keepdims=True))
        p_c = jnp.concatenate([pa, pb], axis=2).astype(BF16)
        o = lax.dot_general(p_c, vv_c, bmm_nn, preferred_element_type=F32)
        o = o + lax.dot_general(pn.astype(BF16), vv_n, bmm_nn, preferred_element_type=F32)
        o = o * jnp.where(lane_low, 1.0 / den_a, 1.0 / den_b)
        for j in range(PAIRS_PER_KV):
            att_s[:, kh * kv_lanes + j * LANES: kh * kv_lanes + (j + 1) * LANES] = (
                o[:, j * dec_seq:(j + 1) * dec_seq, :].reshape(tm, LANES).astype(BF16))


def _attn_kernel(sinks_ref, x_ref, gpre_ref, wq_ref, bq_ref, k_ref, v_ref, kprev_ref, vprev_ref,
                 ck_ref, cv_ref, wo_ref, bo_ref, gpost_ref, xo_ref, q_s, att_s,
                 *, n_prompt_tiles, tiles_per_batch, dec_seq):
    i = pl.program_id(0)
    h = _rms(x_ref[...], gpre_ref[...]).astype(BF16)
    q = jnp.dot(h, wq_ref[...], preferred_element_type=F32) + bq_ref[...]
    q_s[...] = q * (HEAD_DIM ** -0.5)

    @pl.when(i < n_prompt_tiles)
    def _():
        _prompt_attention(i, q_s, att_s, sinks_ref, k_ref[...], kprev_ref[...], v_ref[...], vprev_ref[...],
                          tiles_per_batch)

    @pl.when(i >= n_prompt_tiles)
    def _():
        _sample_attention(q_s, att_s, sinks_ref, k_ref[...], v_ref[...], ck_ref[...], cv_ref[...], dec_seq)

    out = jnp.dot(att_s[...], wo_ref[...], preferred_element_type=F32) + bo_ref[...]
    xo_ref[...] = x_ref[...] + _rms(out, gpost_ref[...])


def _attn(x, k, v, ck, cv, sinks, gpre, wq, bq, wo, bo, gpost, *, n_prompt_rows, prompt_batch, dec_seq):
    rows, d = x.shape
    tm = ROW_TILE
    npt = n_prompt_rows // tm
    tpb = npt // prompt_batch
    blocks_per_tile = tm // CHUNK
    nb = tm // dec_seq
    window = ck.shape[1]
    kern = functools.partial(_attn_kernel, n_prompt_tiles=npt, tiles_per_batch=tpb, dec_seq=dec_seq)
    prev_spec = pl.BlockSpec((CHUNK, LANES), lambda i: (jnp.maximum(i * blocks_per_tile - 1, 0), 0))
    cache_spec = pl.BlockSpec((nb, window, LANES), lambda i: (jnp.maximum(i - npt, 0), 0, 0))
    return pl.pallas_call(
        kern,
        grid=(rows // tm,),
        in_specs=[
            pl.BlockSpec(memory_space=pltpu.SMEM),
            _row_spec(tm, d),
            _const_spec((1, d)),
            _const_spec((d, d)),
            _const_spec((1, d)),
            _row_spec(tm, LANES), _row_spec(tm, LANES), prev_spec, prev_spec,
            cache_spec, cache_spec,
            _const_spec((d, d)),
            _const_spec((1, d)),
            _const_spec((1, d)),
        ],
        out_specs=_row_spec(tm, d),
        out_shape=jax.ShapeDtypeStruct((rows, d), F32),
        scratch_shapes=[pltpu.VMEM((tm, d), F32), pltpu.VMEM((tm, d), BF16)],
        compiler_params=_params(),
        name="swa_attention",
    )(sinks, x, gpre, wq, bq, k, v, k, v, ck, cv, wo, bo, gpost)


def _mixing_weights(w_s, b_s, n):
    layers, groups = w_s.shape[:2]
    r = lax.broadcasted_iota(jnp.int32, (CHUNK, CHUNK), 0)
    c = lax.broadcasted_iota(jnp.int32, (CHUNK, CHUNK), 1)
    w_prompt = jnp.where(r >= c, w_s, 0.0)
    reps = CHUNK // n
    w_tiled = jnp.tile(w_s[:, :, :n, :n], (1, 1, reps, reps))
    w_sample = jnp.where(jnp.logical_and(r >= c, r // n == c // n), w_tiled, 0.0)
    wmix = jnp.stack([w_prompt, w_sample], axis=1).astype(BF16)
    b_prompt = jnp.repeat(jnp.swapaxes(b_s, 1, 2), LANES, axis=2)
    b_sample = jnp.tile(b_prompt[:, :n], (1, reps, 1))
    bmix = jnp.stack([b_prompt, b_sample], axis=1)
    return wmix.reshape(layers * 2, groups, CHUNK, CHUNK), bmix.reshape(layers * 2, CHUNK, -1)


def kernel(x_prompt, x_sample, cache_k, cache_v, sg_norm_pre, sg_w_in, sg_ln_g, sg_ln_b, sg_w_s, sg_b_s, sg_w_out, sg_norm_post, kv_norm, w_kv, b_kv, sw_norm_pre, sw_w_q, sw_b_q, sw_sinks, sw_w_o, sw_b_o, sw_norm_post, f_norm_pre, f_w_gate, f_w_up, f_w_down, f_norm_post):
    bp, sp, d = x_prompt.shape
    bs, ss, _ = x_sample.shape
    n_a = sg_w_in.shape[0]
    n_b = sw_w_q.shape[0]
    depth = n_a + n_b
    n_p, n_s = bp * sp, bs * ss

    wmix, bmix = _mixing_weights(sg_w_s, sg_b_s, ss)
    row = lambda a: a.reshape(1, -1)
    kvh = cache_k.shape[2]
    ck = cache_k.reshape(bs, cache_k.shape[1], -1).astype(BF16)
    cv = cache_v.reshape(bs, cache_v.shape[1], -1).astype(BF16)

    xs = [x_prompt.reshape(n_p, d), x_sample.reshape(n_s, d)]
    v_prompt_rows, v_sample_rows = [], []
    k_all = v_all = y_prompt = y_sample = None
    for layer in range(depth):
        if layer < n_a:
            l = layer
            x, vsp, vss = _mixer(
                xs, row(sg_norm_pre[l]), sg_w_in[l].astype(BF16), row(sg_ln_g[l]), row(sg_ln_b[l]),
                wmix[2 * l:2 * l + 2], bmix[2 * l:2 * l + 2], sg_w_out[l].astype(BF16), row(sg_norm_post[l]),
                n_prompt_rows=n_p, n_sample_rows=n_s, prompt_batch=bp)
            v_prompt_rows.append(vsp)
            v_sample_rows.append(vss.reshape(bs, ss, -1))
        else:
            l = layer - n_a
            x = _attn(x, k_all, v_all, ck, cv, sw_sinks[l], row(sw_norm_pre[l]), sw_w_q[l].astype(BF16),
                      row(sw_b_q[l]), sw_w_o[l].astype(BF16), row(sw_b_o[l]), row(sw_norm_post[l]),
                      n_prompt_rows=n_p, prompt_batch=bp, dec_seq=ss)
        ffn_weights = (row(f_norm_pre[layer]), f_w_gate[layer].astype(BF16), f_w_up[layer].astype(BF16),
                       f_w_down[layer].astype(BF16), row(f_norm_post[layer]))
        if layer == n_a - 1:
            x, k_all, v_all = _ffn(x, *ffn_weights, kv_weights=(row(kv_norm), w_kv.astype(BF16), row(b_kv)))
        elif layer == depth - 1:
            y_prompt = _ffn(x, *ffn_weights, first_row=0, n_rows=n_p)
            y_sample = _ffn(x, *ffn_weights, first_row=n_p, n_rows=n_s)
        else:
            x = _ffn(x, *ffn_weights)
        xs = [x]

    def last_window(a):
        return a[:n_p].reshape(bp, sp, -1)[:, sp - CHUNK:].reshape(bp, CHUNK, kvh, -1)

    return (y_prompt.reshape(bp, sp, d), y_sample.reshape(bs, ss, d),
            jnp.stack(v_prompt_rows, axis=0), jnp.stack(v_sample_rows, axis=0),
            last_window(k_all), last_window(v_all),
            k_all[n_p:].reshape(bs, ss, kvh, -1), v_all[n_p:].reshape(bs, ss, kvh, -1))
```

```python
import functools

import jax
import jax.numpy as jnp
from jax import lax
from jax.experimental import pallas as pl
from jax.experimental.pallas import tpu as pltpu

F32 = jnp.float32
BF16 = jnp.bfloat16

RMS_EPS = 1e-6
LN_EPS = 1e-5
CHUNK = 128
LANES = 128
HEAD_DIM = 64
N_KV_HEADS = 2
HEADS_PER_KV = 8
PAIRS_PER_KV = HEADS_PER_KV // 2
ROW_TILE = 512
VMEM_LIMIT = 56 * 1024 * 1024
NEG_INF = float("-inf")


def _rms(x, g):
    ms = jnp.mean(x * x, axis=-1, keepdims=True)
    return x * lax.rsqrt(ms + RMS_EPS) * g


def _const_spec(shape):
    zeros = (0,) * len(shape)
    return pl.BlockSpec(shape, lambda i: zeros, pipeline_mode=pl.Buffered(1))


def _layer_spec(stacked, layer):
    tail = (0,) * (stacked.ndim - 1)
    return pl.BlockSpec((None,) + stacked.shape[1:], lambda i: (layer,) + tail, pipeline_mode=pl.Buffered(1))


def _row_spec(tm, width, first_tile=0):
    return pl.BlockSpec((tm, width), lambda i: (i + first_tile, 0))


def _params():
    return pltpu.CompilerParams(dimension_semantics=("arbitrary",), vmem_limit_bytes=VMEM_LIMIT)


def _mixer_kernel(*refs, n_prompt_tiles, split_input):
    n_x = 2 if split_input else 1
    gpre_ref, win_ref, lng_ref, lnb_ref, wmix_ref, bmix_ref, wout_ref, gpost_ref = refs[n_x:n_x + 8]
    xo_ref, vsp_ref, vss_ref, y_s = refs[n_x + 8:]
    i = pl.program_id(0)
    tm, width = y_s.shape
    half = tm // 2

    def project_in(rows):
        if split_input:
            x = jnp.where(i < n_prompt_tiles, refs[0][rows, :], refs[1][rows, :])
        else:
            x = refs[0][rows, :]
        h = _rms(x, gpre_ref[...]).astype(BF16)
        return x, jnp.dot(h, win_ref[...], preferred_element_type=F32)

    def gate_and_norm(rows, z):
        z = 0.5 * z * (1.0 + lax.erf(z * (0.5 ** 0.5)))
        u = z[:, :width]
        v = z[:, width:]
        mu = jnp.mean(v, axis=-1, keepdims=True)
        vc = v - mu
        var = jnp.mean(vc * vc, axis=-1, keepdims=True)
        v = vc * lax.rsqrt(var + LN_EPS) * lng_ref[...] + lnb_ref[...]
        if rows.stop == tm:
            vsp_ref[...] = v[half - CHUNK:, :]
        vss_ref[rows, :] = v
        return u, v.astype(BF16)

    def mix_and_project_out(rows, x, u, vb):
        bias = bmix_ref[...]
        for c in range(half // CHUNK):
            r = slice(c * CHUNK, (c + 1) * CHUNK)
            yr = slice(rows.start + c * CHUNK, rows.start + (c + 1) * CHUNK)
            for g in range(width // LANES):
                cols = slice(g * LANES, (g + 1) * LANES)
                mixed = jnp.dot(wmix_ref[g], vb[r, cols], preferred_element_type=F32)
                y_s[yr, cols] = (u[r, cols] * (mixed + bias[:, cols])).astype(BF16)
        out = jnp.dot(y_s[rows, :], wout_ref[...], preferred_element_type=F32)
        xo_ref[rows, :] = x + _rms(out, gpost_ref[...])

    first, second = slice(0, half), slice(half, tm)
    x1, z1 = project_in(first)
    x2, z2 = project_in(second)
    u1, v1 = gate_and_norm(first, z1)
    mix_and_project_out(first, x1, u1, v1)
    u2, v2 = gate_and_norm(second, z2)
    mix_and_project_out(second, x2, u2, v2)


def _mixer(xs, layer, gpre, win, lng, lnb, wmix, bmix, wout, gpost, *, n_prompt_rows, n_sample_rows,
           prompt_batch):
    d = xs[0].shape[1]
    width = wout.shape[1]
    tm = ROW_TILE
    npt = n_prompt_rows // tm
    tpb = npt // prompt_batch
    groups = wmix.shape[1]
    split_input = len(xs) == 2
    if split_input:
        x_specs = [pl.BlockSpec((tm, d), lambda i: (jnp.minimum(i, npt - 1), 0)),
                   pl.BlockSpec((tm, d), lambda i: (jnp.maximum(i - npt, 0), 0))]
    else:
        x_specs = [_row_spec(tm, d)]
    kern = functools.partial(_mixer_kernel, n_prompt_tiles=npt, split_input=split_input)
    return pl.pallas_call(
        kern,
        grid=((n_prompt_rows + n_sample_rows) // tm,),
        in_specs=x_specs + [
            _const_spec((1, d)),
            _layer_spec(win, layer),
            _const_spec((1, width)),
            _const_spec((1, width)),
            pl.BlockSpec((None, groups, CHUNK, CHUNK), lambda i: (2 * layer + i // npt, 0, 0, 0)),
            pl.BlockSpec((None, CHUNK, width), lambda i: (2 * layer + i // npt, 0, 0)),
            _layer_spec(wout, layer),
            _const_spec((1, d)),
        ],
        out_specs=[
            _row_spec(tm, d),
            pl.BlockSpec((None, CHUNK, width), lambda i: (jnp.minimum(i // tpb, prompt_batch), 0, 0)),
            pl.BlockSpec((tm, width), lambda i: (jnp.maximum(i - npt, 0), 0)),
        ],
        out_shape=[
            jax.ShapeDtypeStruct((n_prompt_rows + n_sample_rows, d), F32),
            jax.ShapeDtypeStruct((prompt_batch + 1, CHUNK, width), F32),
            jax.ShapeDtypeStruct((n_sample_rows, width), F32),
        ],
        scratch_shapes=[pltpu.VMEM((tm, width), BF16)],
        compiler_params=_params(),
        name="gmlp_mixer",
    )(*xs, gpre, win, lng, lnb, wmix, bmix, wout, gpost)


def _ffn_kernel(*refs, with_kv):
    x_ref, gpre_ref, wg_ref, wu_ref, wd_ref, gpost_ref = refs[:6]
    x = x_ref[...]
    h = _rms(x, gpre_ref[...]).astype(BF16)
    g = jnp.dot(h, wg_ref[...], preferred_element_type=F32)
    u = jnp.dot(h, wu_ref[...], preferred_element_type=F32)
    a = (g * (1.0 / (1.0 + jnp.exp(-g))) * u).astype(BF16)
    out = jnp.dot(a, wd_ref[...], preferred_element_type=F32)
    x_new = x + _rms(out, gpost_ref[...])
    if not with_kv:
        refs[6][...] = x_new
        return
    kvg_ref, wkv_ref, bkv_ref, xo_ref, k_ref, v_ref = refs[6:]
    xo_ref[...] = x_new
    hk = _rms(x_new, kvg_ref[...]).astype(BF16)
    kv = jnp.dot(hk, wkv_ref[...], preferred_element_type=F32) + bkv_ref[...]
    half = k_ref.shape[1]
    k_ref[...] = kv[:, :half]
    v_ref[...] = kv[:, half:]


def _ffn(x, layer, gpre, wg, wu, wd, gpost, kv_weights=None, first_row=0, n_rows=None):
    d = x.shape[1]
    tm = ROW_TILE
    n_rows = x.shape[0] if n_rows is None else n_rows
    with_kv = kv_weights is not None
    in_specs = [
        _row_spec(tm, d, first_row // tm),
        _const_spec((1, d)),
        _layer_spec(wg, layer),
        _layer_spec(wu, layer),
        _layer_spec(wd, layer),
        _const_spec((1, d)),
    ]
    out_specs = [_row_spec(tm, d)]
    out_shape = [jax.ShapeDtypeStruct((n_rows, d), F32)]
    args = [x, gpre, wg, wu, wd, gpost]
    if with_kv:
        kvg, wkv, bkv = kv_weights
        half = wkv.shape[1] // 2
        in_specs += [_const_spec((1, d)), _const_spec((d, 2 * half)), _const_spec((1, 2 * half))]
        out_specs += [_row_spec(tm, half)] * 2
        out_shape += [jax.ShapeDtypeStruct((n_rows, half), F32)] * 2
        args += [kvg, wkv, bkv]
    out = pl.pallas_call(
        functools.partial(_ffn_kernel, with_kv=with_kv),
        grid=(n_rows // tm,),
        in_specs=in_specs,
        out_specs=out_specs,
        out_shape=out_shape,
        compiler_params=_params(),
        name="swiglu_ffn_kv" if with_kv else "swiglu_ffn",
    )(*args)
    return out if with_kv else out[0]


def _split_heads(k):
    kr = pltpu.roll(k, HEAD_DIM, k.ndim - 1)
    low = lax.broadcasted_iota(jnp.int32, k.shape, k.ndim - 1) < HEAD_DIM
    zero = jnp.zeros_like(k)
    head0 = (jnp.where(low, k, zero).astype(BF16), jnp.where(low, zero, kr).astype(BF16))
    head1 = (jnp.where(low, kr, zero).astype(BF16), jnp.where(low, zero, k).astype(BF16))
    return head0, head1


def _split_heads_t(v):
    vt = v.T
    vt_swapped = jnp.concatenate([vt[HEAD_DIM:], vt[:HEAD_DIM]], axis=0)
    top = lax.broadcasted_iota(jnp.int32, vt.shape, 0) < HEAD_DIM
    zero = jnp.zeros_like(vt)
    head0 = (jnp.where(top, vt, zero).astype(BF16), jnp.where(top, zero, vt_swapped).astype(BF16))
    head1 = (jnp.where(top, vt_swapped, zero).astype(BF16), jnp.where(top, zero, vt).astype(BF16))
    return head0, head1


def _sink_column(sinks_ref, kh, parity, rows_per_pair, lead):
    pieces = []
    for j in range(PAIRS_PER_KV):
        s = sinks_ref[kh * HEADS_PER_KV + 2 * j + parity]
        pieces.append(jnp.full(lead + (rows_per_pair, 1), s, F32))
    return jnp.concatenate(pieces, axis=len(lead))


def _sink_row(sinks_ref, kh, parity):
    pieces = [jnp.full((1, CHUNK), sinks_ref[kh * HEADS_PER_KV + 2 * j + parity], F32)
              for j in range(PAIRS_PER_KV)]
    return jnp.concatenate(pieces, axis=1)


def _prompt_attention(i, q_s, att_s, sinks_ref, k_own, k_prev, v_own, v_prev, tiles_per_batch,
                      project_q, project_o):
    tm = q_s.shape[0]
    half_blocks = tm // CHUNK // 2
    project_q(slice(0, tm // 2))
    project_q(slice(tm // 2, tm))
    kv_lanes = PAIRS_PER_KV * LANES
    n_q = PAIRS_PER_KV * CHUNK
    k_own = _split_heads(k_own)
    k_prev = _split_heads(k_prev)
    v_own = _split_heads_t(v_own)
    v_prev = _split_heads_t(v_prev)
    slot = lax.broadcasted_iota(jnp.int32, (2 * CHUNK, n_q), 0) & (CHUNK - 1)
    t = lax.broadcasted_iota(jnp.int32, (2 * CHUNK, n_q), 1) & (CHUNK - 1)
    own = slot <= t
    no_prev = jnp.where((i % tiles_per_batch) == 0, NEG_INF, 0.0)
    top = lax.broadcasted_iota(jnp.int32, (LANES, n_q), 0) < HEAD_DIM

    def scores(blk, kh):
        rows = slice(blk * CHUNK, (blk + 1) * CHUNK)
        prow = slice((blk - 1) * CHUNK, blk * CHUNK)
        ka_o, kb_o = k_own[kh][0][rows], k_own[kh][1][rows]
        ka_p, kb_p = k_prev[kh] if blk == 0 else (k_own[kh][0][prow], k_own[kh][1][prow])
        keys = jnp.concatenate([ka_o, kb_o, ka_p, kb_p], axis=0)
        qs = jnp.concatenate(
            [q_s[rows, kh * kv_lanes + j * LANES: kh * kv_lanes + (j + 1) * LANES]
             for j in range(PAIRS_PER_KV)], axis=0).astype(BF16)
        s4 = lax.dot_general(keys, qs, (((1,), (1,)), ((), ())), preferred_element_type=F32)
        s_prev = s4[2 * CHUNK:]
        if blk == 0:
            s_prev = s_prev + no_prev
        return jnp.where(own, s4[:2 * CHUNK], s_prev)

    def finish(blk, kh, s):
        rows = slice(blk * CHUNK, (blk + 1) * CHUNK)
        prow = slice((blk - 1) * CHUNK, blk * CHUNK)
        va_o, vb_o = v_own[kh][0][:, rows], v_own[kh][1][:, rows]
        va_p, vb_p = v_prev[kh] if blk == 0 else (v_own[kh][0][:, prow], v_own[kh][1][:, prow])
        vals_t = jnp.concatenate([va_o, vb_o, va_p, vb_p], axis=1)
        sa, sb = s[:CHUNK], s[CHUNK:]
        sink_a = _sink_row(sinks_ref, kh, 0)
        sink_b = _sink_row(sinks_ref, kh, 1)
        ma = jnp.maximum(jnp.max(sa, axis=0, keepdims=True), sink_a)
        mb = jnp.maximum(jnp.max(sb, axis=0, keepdims=True), sink_b)
        pa = jnp.exp(sa - ma)
        pb = jnp.exp(sb - mb)
        inv_a = 1.0 / (jnp.sum(pa, axis=0, keepdims=True) + jnp.exp(sink_a - ma))
        inv_b = 1.0 / (jnp.sum(pb, axis=0, keepdims=True) + jnp.exp(sink_b - mb))
        p = jnp.concatenate([pa, pb], axis=0)
        p4 = jnp.concatenate([jnp.where(own, p, 0.0), jnp.where(own, 0.0, p)], axis=0).astype(BF16)
        o_t = jnp.dot(vals_t, p4, preferred_element_type=F32)
        o_t = o_t * jnp.where(top, inv_a, inv_b)
        for j in range(PAIRS_PER_KV):
            att_s[rows, kh * kv_lanes + j * LANES: kh * kv_lanes + (j + 1) * LANES] = (
                o_t[:, j * CHUNK:(j + 1) * CHUNK].T.astype(BF16))

    items = [(blk, kh) for blk in range(tm // CHUNK) for kh in range(N_KV_HEADS)]
    s_next = scores(*items[0])
    for n, item in enumerate(items):
        s_cur = s_next
        if n + 1 < len(items):
            s_next = scores(*items[n + 1])
        finish(*item, s_cur)
        blk, kh = item
        if kh == N_KV_HEADS - 1 and (blk + 1) % half_blocks == 0:
            first = (blk + 1 - half_blocks) * CHUNK
            project_o(slice(first, first + tm // 2))


def _sample_attention(q_s, att_s, sinks_ref, k_new, v_new, k_old, v_old, dec_seq):
    tm, d = q_s.shape
    kv_lanes = PAIRS_PER_KV * LANES
    nb = tm // dec_seq
    m_rows = PAIRS_PER_KV * dec_seq
    window = k_old.shape[1]
    k_new = _split_heads(k_new.reshape(nb, dec_seq, LANES))
    v_new = _split_heads(v_new.reshape(nb, dec_seq, LANES))
    k_old = _split_heads(k_old.astype(F32))
    v_old = _split_heads(v_old.astype(F32))
    t_c = lax.broadcasted_iota(jnp.int32, (nb, m_rows, 2 * window), 1) & (dec_seq - 1)
    c_c = lax.broadcasted_iota(jnp.int32, (nb, m_rows, 2 * window), 2) & (window - 1)
    valid_c = c_c > t_c
    t_n = lax.broadcasted_iota(jnp.int32, (nb, m_rows, 2 * dec_seq), 1) & (dec_seq - 1)
    col_n = lax.broadcasted_iota(jnp.int32, (nb, m_rows, 2 * dec_seq), 2)
    valid_n = (col_n & (dec_seq - 1)) <= t_n
    first_n = col_n < dec_seq
    lane_low = lax.broadcasted_iota(jnp.int32, (nb, m_rows, LANES), 2) < HEAD_DIM
    q3 = q_s[...].reshape(nb, dec_seq, d)
    bmm_nt = (((2,), (2,)), ((0,), (0,)))
    bmm_nn = (((2,), (1,)), ((0,), (0,)))
    for kh in range(N_KV_HEADS):
        rhs_c = jnp.concatenate([k_old[kh][0], k_old[kh][1]], axis=1)
        vv_c = jnp.concatenate([v_old[kh][0], v_old[kh][1]], axis=1)
        rhs_n = jnp.concatenate([k_new[kh][0].astype(F32), k_new[kh][1].astype(F32)], axis=1).astype(BF16)
        vv_n = jnp.concatenate([v_new[kh][0].astype(F32), v_new[kh][1].astype(F32)], axis=1).astype(BF16)
        qs = jnp.concatenate(
            [q3[:, :, kh * kv_lanes + j * LANES: kh * kv_lanes + (j + 1) * LANES]
             for j in range(PAIRS_PER_KV)], axis=1).astype(BF16)
        s_c = lax.dot_general(qs, rhs_c, bmm_nt, preferred_element_type=F32)
        s_n = lax.dot_general(qs, rhs_n, bmm_nt, preferred_element_type=F32)
        s_c = jnp.where(valid_c, s_c, NEG_INF)
        s_n = jnp.where(valid_n, s_n, NEG_INF)
        sa, sb = s_c[:, :, :window], s_c[:, :, window:]
        sink_a = _sink_column(sinks_ref, kh, 0, dec_seq, (1,))
        sink_b = _sink_column(sinks_ref, kh, 1, dec_seq, (1,))
        na = jnp.max(jnp.where(first_n, s_n, NEG_INF), axis=2, keepdims=True)
        nb_ = jnp.max(jnp.where(first_n, NEG_INF, s_n), axis=2, keepdims=True)
        ma = jnp.maximum(jnp.maximum(jnp.max(sa, axis=2, keepdims=True), na), sink_a)
        mb = jnp.maximum(jnp.maximum(jnp.max(sb, axis=2, keepdims=True), nb_), sink_b)
        pa = jnp.exp(sa - ma)
        pb = jnp.exp(sb - mb)
        pn = jnp.exp(s_n - jnp.where(first_n, ma, mb))
        den_a = (jnp.sum(pa, axis=2, keepdims=True) + jnp.exp(sink_a - ma)
                 + jnp.sum(jnp.where(first_n, pn, 0.0), axis=2, keepdims=True))
        den_b = (jnp.sum(pb, axis=2, keepdims=True) + jnp.exp(sink_b - mb)
                 + jnp.sum(jnp.where(first_n, 0.0, pn), axis=2, keepdims=True))
        p_c = jnp.concatenate([pa, pb], axis=2).astype(BF16)
        o = lax.dot_general(p_c, vv_c, bmm_nn, preferred_element_type=F32)
        o = o + lax.dot_general(pn.astype(BF16), vv_n, bmm_nn, preferred_element_type=F32)
        o = o * jnp.where(lane_low, 1.0 / den_a, 1.0 / den_b)
        for j in range(PAIRS_PER_KV):
            att_s[:, kh * kv_lanes + j * LANES: kh * kv_lanes + (j + 1) * LANES] = (
                o[:, j * dec_seq:(j + 1) * dec_seq, :].reshape(tm, LANES).astype(BF16))


def _attn_kernel(sinks_ref, x_ref, gpre_ref, wq_ref, bq_ref, k_ref, v_ref, kprev_ref, vprev_ref,
                 ck_ref, cv_ref, wo_ref, bo_ref, gpost_ref, xo_ref, q_s, att_s,
                 *, n_prompt_tiles, tiles_per_batch, dec_seq):
    i = pl.program_id(0)
    tm = q_s.shape[0]

    def project_q(rows):
        h = _rms(x_ref[rows, :], gpre_ref[...]).astype(BF16)
        q = jnp.dot(h, wq_ref[...], preferred_element_type=F32) + bq_ref[...]
        q_s[rows, :] = q * (HEAD_DIM ** -0.5)

    def project_o(rows):
        out = jnp.dot(att_s[rows, :], wo_ref[...], preferred_element_type=F32) + bo_ref[...]
        xo_ref[rows, :] = x_ref[rows, :] + _rms(out, gpost_ref[...])

    @pl.when(i < n_prompt_tiles)
    def _():
        _prompt_attention(i, q_s, att_s, sinks_ref, k_ref[...], kprev_ref[...], v_ref[...], vprev_ref[...],
                          tiles_per_batch, project_q, project_o)

    @pl.when(i >= n_prompt_tiles)
    def _():
        project_q(slice(0, tm))
        _sample_attention(q_s, att_s, sinks_ref, k_ref[...], v_ref[...], ck_ref[...], cv_ref[...], dec_seq)
        project_o(slice(0, tm))


def _attn(x, layer, k, v, ck, cv, sinks, gpre, wq, bq, wo, bo, gpost, *, n_prompt_rows, prompt_batch, dec_seq):
    rows, d = x.shape
    tm = ROW_TILE
    npt = n_prompt_rows // tm
    tpb = npt // prompt_batch
    blocks_per_tile = tm // CHUNK
    nb = tm // dec_seq
    window = ck.shape[1]
    kern = functools.partial(_attn_kernel, n_prompt_tiles=npt, tiles_per_batch=tpb, dec_seq=dec_seq)
    prev_spec = pl.BlockSpec((CHUNK, LANES), lambda i: (jnp.maximum(i * blocks_per_tile - 1, 0), 0))
    cache_spec = pl.BlockSpec((nb, window, LANES), lambda i: (jnp.maximum(i - npt, 0), 0, 0))
    return pl.pallas_call(
        kern,
        grid=(rows // tm,),
        in_specs=[
            pl.BlockSpec(memory_space=pltpu.SMEM),
            _row_spec(tm, d),
            _const_spec((1, d)),
            _layer_spec(wq, layer),
            _const_spec((1, d)),
            _row_spec(tm, LANES), _row_spec(tm, LANES), prev_spec, prev_spec,
            cache_spec, cache_spec,
            _layer_spec(wo, layer),
            _const_spec((1, d)),
            _const_spec((1, d)),
        ],
        out_specs=_row_spec(tm, d),
        out_shape=jax.ShapeDtypeStruct((rows, d), F32),
        scratch_shapes=[pltpu.VMEM((tm, d), F32), pltpu.VMEM((tm, d), BF16)],
        compiler_params=_params(),
        name="swa_attention",
    )(sinks, x, gpre, wq, bq, k, v, k, v, ck, cv, wo, bo, gpost)


def _mixing_weights(w_s, b_s, n):
    layers, groups = w_s.shape[:2]
    r = lax.broadcasted_iota(jnp.int32, (CHUNK, CHUNK), 0)
    c = lax.broadcasted_iota(jnp.int32, (CHUNK, CHUNK), 1)
    w_prompt = jnp.where(r >= c, w_s, 0.0)
    reps = CHUNK // n
    pick = (lax.broadcasted_iota(jnp.int32, (CHUNK, n), 0) % n
            == lax.broadcasted_iota(jnp.int32, (CHUNK, n), 1)).astype(F32)
    w_tiled = jnp.einsum('ra,lgab,cb->lgrc', pick, w_s[:, :, :n, :n], pick, precision=lax.Precision.HIGHEST)
    w_sample = jnp.where(jnp.logical_and(r >= c, r // n == c // n), w_tiled, 0.0)
    wmix = jnp.stack([w_prompt, w_sample], axis=1).astype(BF16)
    b_prompt = jnp.repeat(jnp.swapaxes(b_s, 1, 2), LANES, axis=2)
    b_sample = jnp.tile(b_prompt[:, :n], (1, reps, 1))
    bmix = jnp.stack([b_prompt, b_sample], axis=1)
    return wmix.reshape(layers * 2, groups, CHUNK, CHUNK), bmix.reshape(layers * 2, CHUNK, -1)


def kernel(x_prompt, x_sample, cache_k, cache_v, sg_norm_pre, sg_w_in, sg_ln_g, sg_ln_b, sg_w_s, sg_b_s, sg_w_out, sg_norm_post, kv_norm, w_kv, b_kv, sw_norm_pre, sw_w_q, sw_b_q, sw_sinks, sw_w_o, sw_b_o, sw_norm_post, f_norm_pre, f_w_gate, f_w_up, f_w_down, f_norm_post):
    bp, sp, d = x_prompt.shape
    bs, ss, _ = x_sample.shape
    n_a = sg_w_in.shape[0]
    n_b = sw_w_q.shape[0]
    depth = n_a + n_b
    n_p, n_s = bp * sp, bs * ss

    wmix, bmix = _mixing_weights(sg_w_s, sg_b_s, ss)
    row = lambda a: a.reshape(1, -1)
    kvh = cache_k.shape[2]
    ck = cache_k.reshape(bs, cache_k.shape[1], -1).astype(BF16)
    cv = cache_v.reshape(bs, cache_v.shape[1], -1).astype(BF16)

    w_in, w_out = sg_w_in.astype(BF16), sg_w_out.astype(BF16)
    w_q, w_o = sw_w_q.astype(BF16), sw_w_o.astype(BF16)
    w_gate, w_up, w_down = f_w_gate.astype(BF16), f_w_up.astype(BF16), f_w_down.astype(BF16)

    xs = [x_prompt.reshape(n_p, d), x_sample.reshape(n_s, d)]
    v_prompt_rows, v_sample_rows = [], []
    k_all = v_all = y_prompt = y_sample = None
    for layer in range(depth):
        if layer < n_a:
            l = layer
            x, vsp, vss = _mixer(
                xs, l, row(sg_norm_pre[l]), w_in, row(sg_ln_g[l]), row(sg_ln_b[l]), wmix, bmix, w_out,
                row(sg_norm_post[l]), n_prompt_rows=n_p, n_sample_rows=n_s, prompt_batch=bp)
            v_prompt_rows.append(vsp[:bp])
            v_sample_rows.append(vss.reshape(bs, ss, -1))
        else:
            l = layer - n_a
            x = _attn(x, l, k_all, v_all, ck, cv, sw_sinks[l], row(sw_norm_pre[l]), w_q, row(sw_b_q[l]), w_o,
                      row(sw_b_o[l]), row(sw_norm_post[l]), n_prompt_rows=n_p, prompt_batch=bp, dec_seq=ss)
        ffn_weights = (layer, row(f_norm_pre[layer]), w_gate, w_up, w_down, row(f_norm_post[layer]))
        if layer == n_a - 1:
            x, k_all, v_all = _ffn(x, *ffn_weights, kv_weights=(row(kv_norm), w_kv.astype(BF16), row(b_kv)))
        elif layer == depth - 1:
            y_prompt = _ffn(x, *ffn_weights, first_row=0, n_rows=n_p)
            y_sample = _ffn(x, *ffn_weights, first_row=n_p, n_rows=n_s)
        else:
            x = _ffn(x, *ffn_weights)
        xs = [x]

    def last_window(a):
        return a[:n_p].reshape(bp, sp, -1)[:, sp - CHUNK:].reshape(bp, CHUNK, kvh, -1)

    return (y_prompt.reshape(bp, sp, d), y_sample.reshape(bs, ss, d),
            jnp.stack(v_prompt_rows, axis=0), jnp.stack(v_sample_rows, axis=0),
            last_window(k_all), last_window(v_all),
            k_all[n_p:].reshape(bs, ss, kvh, -1), v_all[n_p:].reshape(bs, ss, kvh, -1))
```

```python
import functools

import jax
import jax.numpy as jnp
from jax import lax
from jax.experimental import pallas as pl
from jax.experimental.pallas import tpu as pltpu

F32 = jnp.float32
BF16 = jnp.bfloat16

RMS_EPS = 1e-6
LN_EPS = 1e-5
CHUNK = 128
LANES = 128
HEAD_DIM = 64
N_KV_HEADS = 2
HEADS_PER_KV = 8
PAIRS_PER_KV = HEADS_PER_KV // 2
ROW_TILE = 512
WIDE_TILE = 1024
SUB_TILE = 256
VMEM_LIMIT = 56 * 1024 * 1024
NEG_INF = float("-inf")


def _rms(x, g):
    ms = jnp.mean(x * x, axis=-1, keepdims=True)
    return x * lax.rsqrt(ms + RMS_EPS) * g


def _const_spec(shape):
    zeros = (0,) * len(shape)
    return pl.BlockSpec(shape, lambda i: zeros, pipeline_mode=pl.Buffered(1))


def _layer_spec(stacked, layer):
    tail = (0,) * (stacked.ndim - 1)
    return pl.BlockSpec((None,) + stacked.shape[1:], lambda i: (layer,) + tail, pipeline_mode=pl.Buffered(1))


def _row_spec(tm, width, first_tile=0):
    return pl.BlockSpec((tm, width), lambda i: (i + first_tile, 0))


def _params():
    return pltpu.CompilerParams(dimension_semantics=("arbitrary",), vmem_limit_bytes=VMEM_LIMIT)


def _mixer_kernel(*refs, n_prompt_tiles, split_input):
    n_x = 2 if split_input else 1
    gpre_ref, win_ref, lng_ref, lnb_ref, wmix_ref, bmix_ref, wout_ref, gpost_ref = refs[n_x:n_x + 8]
    xo_ref, vsp_ref, vss_ref, y_s = refs[n_x + 8:]
    i = pl.program_id(0)
    tm, width = y_s.shape
    sub = SUB_TILE

    def project_in(rows):
        if split_input:
            x = jnp.where(i < n_prompt_tiles, refs[0][rows, :], refs[1][rows, :])
        else:
            x = refs[0][rows, :]
        h = _rms(x, gpre_ref[...]).astype(BF16)
        return x, jnp.dot(h, win_ref[...], preferred_element_type=F32)

    def gate_and_norm(rows, z):
        z = 0.5 * z * (1.0 + lax.erf(z * (0.5 ** 0.5)))
        u = z[:, :width]
        v = z[:, width:]
        mu = jnp.mean(v, axis=-1, keepdims=True)
        vc = v - mu
        var = jnp.mean(vc * vc, axis=-1, keepdims=True)
        v = vc * lax.rsqrt(var + LN_EPS) * lng_ref[...] + lnb_ref[...]
        if rows.stop == tm:
            vsp_ref[...] = v[sub - CHUNK:, :]
        vss_ref[rows, :] = v
        return u, v.astype(BF16)

    def mix_and_project_out(rows, x, u, vb):
        bias = bmix_ref[...]
        for c in range(sub // CHUNK):
            r = slice(c * CHUNK, (c + 1) * CHUNK)
            yr = slice(rows.start + c * CHUNK, rows.start + (c + 1) * CHUNK)
            for g in range(width // LANES):
                cols = slice(g * LANES, (g + 1) * LANES)
                mixed = jnp.dot(wmix_ref[g], vb[r, cols], preferred_element_type=F32)
                y_s[yr, cols] = (u[r, cols] * (mixed + bias[:, cols])).astype(BF16)
        out = jnp.dot(y_s[rows, :], wout_ref[...], preferred_element_type=F32)
        xo_ref[rows, :] = x + _rms(out, gpost_ref[...])

    parts = [slice(s, s + sub) for s in range(0, tm, sub)]
    ahead = project_in(parts[0])
    for n, rows in enumerate(parts):
        x, z = ahead
        if n + 1 < len(parts):
            ahead = project_in(parts[n + 1])
        u, vb = gate_and_norm(rows, z)
        mix_and_project_out(rows, x, u, vb)


def _mixer(xs, layer, gpre, win, lng, lnb, wmix, bmix, wout, gpost, *, n_prompt_rows, n_sample_rows,
           prompt_batch):
    d = xs[0].shape[1]
    width = wout.shape[1]
    tm = WIDE_TILE
    npt = n_prompt_rows // tm
    tpb = npt // prompt_batch
    groups = wmix.shape[1]
    split_input = len(xs) == 2
    if split_input:
        x_specs = [pl.BlockSpec((tm, d), lambda i: (jnp.minimum(i, npt - 1), 0)),
                   pl.BlockSpec((tm, d), lambda i: (jnp.maximum(i - npt, 0), 0))]
    else:
        x_specs = [_row_spec(tm, d)]
    kern = functools.partial(_mixer_kernel, n_prompt_tiles=npt, split_input=split_input)
    return pl.pallas_call(
        kern,
        grid=((n_prompt_rows + n_sample_rows) // tm,),
        in_specs=x_specs + [
            _const_spec((1, d)),
            _layer_spec(win, layer),
            _const_spec((1, width)),
            _const_spec((1, width)),
            pl.BlockSpec((None, groups, CHUNK, CHUNK), lambda i: (2 * layer + i // npt, 0, 0, 0)),
            pl.BlockSpec((None, CHUNK, width), lambda i: (2 * layer + i // npt, 0, 0)),
            _layer_spec(wout, layer),
            _const_spec((1, d)),
        ],
        out_specs=[
            _row_spec(tm, d),
            pl.BlockSpec((None, CHUNK, width), lambda i: (jnp.minimum(i // tpb, prompt_batch), 0, 0)),
            pl.BlockSpec((tm, width), lambda i: (jnp.maximum(i - npt, 0), 0)),
        ],
        out_shape=[
            jax.ShapeDtypeStruct((n_prompt_rows + n_sample_rows, d), F32),
            jax.ShapeDtypeStruct((prompt_batch + 1, CHUNK, width), F32),
            jax.ShapeDtypeStruct((n_sample_rows, width), F32),
        ],
        scratch_shapes=[pltpu.VMEM((tm, width), BF16)],
        compiler_params=_params(),
        name="gmlp_mixer",
    )(*xs, gpre, win, lng, lnb, wmix, bmix, wout, gpost)


def _ffn_kernel(*refs, with_kv):
    x_ref, gpre_ref, wg_ref, wu_ref, wd_ref, gpost_ref = refs[:6]
    if with_kv:
        kvg_ref, wkv_ref, bkv_ref, xo_ref, k_ref, v_ref = refs[6:]
    else:
        xo_ref = refs[6]
    tm = x_ref.shape[0]

    def project_up(rows):
        x = x_ref[rows, :]
        h = _rms(x, gpre_ref[...]).astype(BF16)
        g = jnp.dot(h, wg_ref[...], preferred_element_type=F32)
        u = jnp.dot(h, wu_ref[...], preferred_element_type=F32)
        return x, g, u

    def gate_and_project_down(rows, x, g, u):
        a = (g * (1.0 / (1.0 + jnp.exp(-g))) * u).astype(BF16)
        out = jnp.dot(a, wd_ref[...], preferred_element_type=F32)
        x_new = x + _rms(out, gpost_ref[...])
        xo_ref[rows, :] = x_new
        if with_kv:
            hk = _rms(x_new, kvg_ref[...]).astype(BF16)
            kv = jnp.dot(hk, wkv_ref[...], preferred_element_type=F32) + bkv_ref[...]
            half = k_ref.shape[1]
            k_ref[rows, :] = kv[:, :half]
            v_ref[rows, :] = kv[:, half:]

    parts = [slice(s, s + SUB_TILE) for s in range(0, tm, SUB_TILE)]
    ahead = project_up(parts[0])
    for n, rows in enumerate(parts):
        x, g, u = ahead
        if n + 1 < len(parts):
            ahead = project_up(parts[n + 1])
        gate_and_project_down(rows, x, g, u)


def _ffn(x, layer, gpre, wg, wu, wd, gpost, kv_weights=None, first_row=0, n_rows=None):
    d = x.shape[1]
    tm = WIDE_TILE
    n_rows = x.shape[0] if n_rows is None else n_rows
    with_kv = kv_weights is not None
    in_specs = [
        _row_spec(tm, d, first_row // tm),
        _const_spec((1, d)),
        _layer_spec(wg, layer),
        _layer_spec(wu, layer),
        _layer_spec(wd, layer),
        _const_spec((1, d)),
    ]
    out_specs = [_row_spec(tm, d)]
    out_shape = [jax.ShapeDtypeStruct((n_rows, d), F32)]
    args = [x, gpre, wg, wu, wd, gpost]
    if with_kv:
        kvg, wkv, bkv = kv_weights
        half = wkv.shape[1] // 2
        in_specs += [_const_spec((1, d)), _const_spec((d, 2 * half)), _const_spec((1, 2 * half))]
        out_specs += [_row_spec(tm, half)] * 2
        out_shape += [jax.ShapeDtypeStruct((n_rows, half), F32)] * 2
        args += [kvg, wkv, bkv]
    out = pl.pallas_call(
        functools.partial(_ffn_kernel, with_kv=with_kv),
        grid=(n_rows // tm,),
        in_specs=in_specs,
        out_specs=out_specs,
        out_shape=out_shape,
        compiler_params=_params(),
        name="swiglu_ffn_kv" if with_kv else "swiglu_ffn",
    )(*args)
    return out if with_kv else out[0]


def _split_heads(k):
    kr = pltpu.roll(k, HEAD_DIM, k.ndim - 1)
    low = lax.broadcasted_iota(jnp.int32, k.shape, k.ndim - 1) < HEAD_DIM
    zero = jnp.zeros_like(k)
    head0 = (jnp.where(low, k, zero).astype(BF16), jnp.where(low, zero, kr).astype(BF16))
    head1 = (jnp.where(low, kr, zero).astype(BF16), jnp.where(low, zero, k).astype(BF16))
    return head0, head1


def _split_heads_t(v):
    vt = v.T
    vt_swapped = jnp.concatenate([vt[HEAD_DIM:], vt[:HEAD_DIM]], axis=0)
    top = lax.broadcasted_iota(jnp.int32, vt.shape, 0) < HEAD_DIM
    zero = jnp.zeros_like(vt)
    head0 = (jnp.where(top, vt, zero).astype(BF16), jnp.where(top, zero, vt_swapped).astype(BF16))
    head1 = (jnp.where(top, vt_swapped, zero).astype(BF16), jnp.where(top, zero, vt).astype(BF16))
    return head0, head1


def _sink_column(sinks_ref, kh, parity, rows_per_pair, lead):
    pieces = []
    for j in range(PAIRS_PER_KV):
        s = sinks_ref[kh * HEADS_PER_KV + 2 * j + parity]
        pieces.append(jnp.full(lead + (rows_per_pair, 1), s, F32))
    return jnp.concatenate(pieces, axis=len(lead))


def _sink_row(sinks_ref, kh, parity):
    pieces = [jnp.full((1, CHUNK), sinks_ref[kh * HEADS_PER_KV + 2 * j + parity], F32)
              for j in range(PAIRS_PER_KV)]
    return jnp.concatenate(pieces, axis=1)


def _prompt_attention(i, q_s, att_s, sinks_ref, k_own, k_prev, v_own, v_prev, tiles_per_batch,
                      project_q, project_o):
    tm = q_s.shape[0]
    half_blocks = tm // CHUNK // 2
    project_q(slice(0, tm // 2))
    project_q(slice(tm // 2, tm))
    kv_lanes = PAIRS_PER_KV * LANES
    n_q = PAIRS_PER_KV * CHUNK
    k_own = _split_heads(k_own)
    k_prev = _split_heads(k_prev)
    v_own = _split_heads_t(v_own)
    v_prev = _split_heads_t(v_prev)
    slot = lax.broadcasted_iota(jnp.int32, (2 * CHUNK, n_q), 0) & (CHUNK - 1)
    t = lax.broadcasted_iota(jnp.int32, (2 * CHUNK, n_q), 1) & (CHUNK - 1)
    own = slot <= t
    no_prev = jnp.where((i % tiles_per_batch) == 0, NEG_INF, 0.0)
    top = lax.broadcasted_iota(jnp.int32, (LANES, n_q), 0) < HEAD_DIM

    def scores(blk, kh):
        rows = slice(blk * CHUNK, (blk + 1) * CHUNK)
        prow = slice((blk - 1) * CHUNK, blk * CHUNK)
        ka_o, kb_o = k_own[kh][0][rows], k_own[kh][1][rows]
        ka_p, kb_p = k_prev[kh] if blk == 0 else (k_own[kh][0][prow], k_own[kh][1][prow])
        keys = jnp.concatenate([ka_o, kb_o, ka_p, kb_p], axis=0)
        qs = jnp.concatenate(
            [q_s[rows, kh * kv_lanes + j * LANES: kh * kv_lanes + (j + 1) * LANES]
             for j in range(PAIRS_PER_KV)], axis=0).astype(BF16)
        s4 = lax.dot_general(keys, qs, (((1,), (1,)), ((), ())), preferred_element_type=F32)
        s_prev = s4[2 * CHUNK:]
        if blk == 0:
            s_prev = s_prev + no_prev
        return jnp.where(own, s4[:2 * CHUNK], s_prev)

    def finish(blk, kh, s):
        rows = slice(blk * CHUNK, (blk + 1) * CHUNK)
        prow = slice((blk - 1) * CHUNK, blk * CHUNK)
        va_o, vb_o = v_own[kh][0][:, rows], v_own[kh][1][:, rows]
        va_p, vb_p = v_prev[kh] if blk == 0 else (v_own[kh][0][:, prow], v_own[kh][1][:, prow])
        vals_t = jnp.concatenate([va_o, vb_o, va_p, vb_p], axis=1)
        sa, sb = s[:CHUNK], s[CHUNK:]
        sink_a = _sink_row(sinks_ref, kh, 0)
        sink_b = _sink_row(sinks_ref, kh, 1)
        ma = jnp.maximum(jnp.max(sa, axis=0, keepdims=True), sink_a)
        mb = jnp.maximum(jnp.max(sb, axis=0, keepdims=True), sink_b)
        pa = jnp.exp(sa - ma)
        pb = jnp.exp(sb - mb)
        inv_a = 1.0 / (jnp.sum(pa, axis=0, keepdims=True) + jnp.exp(sink_a - ma))
        inv_b = 1.0 / (jnp.sum(pb, axis=0, keepdims=True) + jnp.exp(sink_b - mb))
        p = jnp.concatenate([pa, pb], axis=0)
        p4 = jnp.concatenate([jnp.where(own, p, 0.0), jnp.where(own, 0.0, p)], axis=0).astype(BF16)
        o_t = jnp.dot(vals_t, p4, preferred_element_type=F32)
        o_t = o_t * jnp.where(top, inv_a, inv_b)
        for j in range(PAIRS_PER_KV):
            att_s[rows, kh * kv_lanes + j * LANES: kh * kv_lanes + (j + 1) * LANES] = (
                o_t[:, j * CHUNK:(j + 1) * CHUNK].T.astype(BF16))

    items = [(blk, kh) for blk in range(tm // CHUNK) for kh in range(N_KV_HEADS)]
    s_next = scores(*items[0])
    for n, item in enumerate(items):
        s_cur = s_next
        if n + 1 < len(items):
            s_next = scores(*items[n + 1])
        finish(*item, s_cur)
        blk, kh = item
        if kh == N_KV_HEADS - 1 and (blk + 1) % half_blocks == 0:
            first = (blk + 1 - half_blocks) * CHUNK
            project_o(slice(first, first + tm // 2))


def _sample_attention(q_s, att_s, sinks_ref, k_new, v_new, k_old, v_old, dec_seq):
    tm, d = q_s.shape
    kv_lanes = PAIRS_PER_KV * LANES
    nb = tm // dec_seq
    m_rows = PAIRS_PER_KV * dec_seq
    window = k_old.shape[1]
    k_new = _split_heads(k_new.reshape(nb, dec_seq, LANES))
    v_new = _split_heads(v_new.reshape(nb, dec_seq, LANES))
    k_old = _split_heads(k_old.astype(F32))
    v_old = _split_heads(v_old.astype(F32))
    t_c = lax.broadcasted_iota(jnp.int32, (nb, m_rows, 2 * window), 1) & (dec_seq - 1)
    c_c = lax.broadcasted_iota(jnp.int32, (nb, m_rows, 2 * window), 2) & (window - 1)
    valid_c = c_c > t_c
    t_n = lax.broadcasted_iota(jnp.int32, (nb, m_rows, 2 * dec_seq), 1) & (dec_seq - 1)
    col_n = lax.broadcasted_iota(jnp.int32, (nb, m_rows, 2 * dec_seq), 2)
    valid_n = (col_n & (dec_seq - 1)) <= t_n
    first_n = col_n < dec_seq
    lane_low = lax.broadcasted_iota(jnp.int32, (nb, m_rows, LANES), 2) < HEAD_DIM
    q3 = q_s[...].reshape(nb, dec_seq, d)
    bmm_nt = (((2,), (2,)), ((0,), (0,)))
    bmm_nn = (((2,), (1,)), ((0,), (0,)))
    for kh in range(N_KV_HEADS):
        rhs_c = jnp.concatenate([k_old[kh][0], k_old[kh][1]], axis=1)
        vv_c = jnp.concatenate([v_old[kh][0], v_old[kh][1]], axis=1)
        rhs_n = jnp.concatenate([k_new[kh][0].astype(F32), k_new[kh][1].astype(F32)], axis=1).astype(BF16)
        vv_n = jnp.concatenate([v_new[kh][0].astype(F32), v_new[kh][1].astype(F32)], axis=1).astype(BF16)
        qs = jnp.concatenate(
            [q3[:, :, kh * kv_lanes + j * LANES: kh * kv_lanes + (j + 1) * LANES]
             for j in range(PAIRS_PER_KV)], axis=1).astype(BF16)
        s_c = lax.dot_general(qs, rhs_c, bmm_nt, preferred_element_type=F32)
        s_n = lax.dot_general(qs, rhs_n, bmm_nt, preferred_element_type=F32)
        s_c = jnp.where(valid_c, s_c, NEG_INF)
        s_n = jnp.where(valid_n, s_n, NEG_INF)
        sa, sb = s_c[:, :, :window], s_c[:, :, window:]
        sink_a = _sink_column(sinks_ref, kh, 0, dec_seq, (1,))
        sink_b = _sink_column(sinks_ref, kh, 1, dec_seq, (1,))
        na = jnp.max(jnp.where(first_n, s_n, NEG_INF), axis=2, keepdims=True)
        nb_ = jnp.max(jnp.where(first_n, NEG_INF, s_n), axis=2, keepdims=True)
        ma = jnp.maximum(jnp.maximum(jnp.max(sa, axis=2, keepdims=True), na), sink_a)
        mb = jnp.maximum(jnp.maximum(jnp.max(sb, axis=2, keepdims=True), nb_), sink_b)
        pa = jnp.exp(sa - ma)
        pb = jnp.exp(sb - mb)
        pn = jnp.exp(s_n - jnp.where(first_n, ma, mb))
        den_a = (jnp.sum(pa, axis=2, keepdims=True) + jnp.exp(sink_a - ma)
                 + jnp.sum(jnp.where(first_n, pn, 0.0), axis=2, keepdims=True))
        den_b = (jnp.sum(pb, axis=2, keepdims=True) + jnp.exp(sink_b - mb)
                 + jnp.sum(jnp.where(first_n, 0.0, pn), axis=2, keepdims=True))
        p_c = jnp.concatenate([pa, pb], axis=2).astype(BF16)
        o = lax.dot_general(p_c, vv_c, bmm_nn, preferred_element_type=F32)
        o = o + lax.dot_general(pn.astype(BF16), vv_n, bmm_nn, preferred_element_type=F32)
        o = o * jnp.where(lane_low, 1.0 / den_a, 1.0 / den_b)
        for j in range(PAIRS_PER_KV):
            att_s[:, kh * kv_lanes + j * LANES: kh * kv_lanes + (j + 1) * LANES] = (
                o[:, j * dec_seq:(j + 1) * dec_seq, :].reshape(tm, LANES).astype(BF16))


def _attn_kernel(sinks_ref, x_ref, gpre_ref, wq_ref, bq_ref, k_ref, v_ref, kprev_ref, vprev_ref,
                 ck_ref, cv_ref, wo_ref, bo_ref, gpost_ref, xo_ref, q_s, att_s,
                 *, n_prompt_tiles, tiles_per_batch, dec_seq):
    i = pl.program_id(0)
    tm = q_s.shape[0]

    def project_q(rows):
        h = _rms(x_ref[rows, :], gpre_ref[...]).astype(BF16)
        q = jnp.dot(h, wq_ref[...], preferred_element_type=F32) + bq_ref[...]
        q_s[rows, :] = q * (HEAD_DIM ** -0.5)

    def project_o(rows):
        out = jnp.dot(att_s[rows, :], wo_ref[...], preferred_element_type=F32) + bo_ref[...]
        xo_ref[rows, :] = x_ref[rows, :] + _rms(out, gpost_ref[...])

    @pl.when(i < n_prompt_tiles)
    def _():
        _prompt_attention(i, q_s, att_s, sinks_ref, k_ref[...], kprev_ref[...], v_ref[...], vprev_ref[...],
                          tiles_per_batch, project_q, project_o)

    @pl.when(i >= n_prompt_tiles)
    def _():
        project_q(slice(0, tm))
        _sample_attention(q_s, att_s, sinks_ref, k_ref[...], v_ref[...], ck_ref[...], cv_ref[...], dec_seq)
        project_o(slice(0, tm))


def _attn(x, layer, k, v, ck, cv, sinks, gpre, wq, bq, wo, bo, gpost, *, n_prompt_rows, prompt_batch, dec_seq):
    rows, d = x.shape
    tm = ROW_TILE
    npt = n_prompt_rows // tm
    tpb = npt // prompt_batch
    blocks_per_tile = tm // CHUNK
    nb = tm // dec_seq
    window = ck.shape[1]
    kern = functools.partial(_attn_kernel, n_prompt_tiles=npt, tiles_per_batch=tpb, dec_seq=dec_seq)
    prev_spec = pl.BlockSpec((CHUNK, LANES), lambda i: (jnp.maximum(i * blocks_per_tile - 1, 0), 0))
    cache_spec = pl.BlockSpec((nb, window, LANES), lambda i: (jnp.maximum(i - npt, 0), 0, 0))
    return pl.pallas_call(
        kern,
        grid=(rows // tm,),
        in_specs=[
            pl.BlockSpec(memory_space=pltpu.SMEM),
            _row_spec(tm, d),
            _const_spec((1, d)),
            _layer_spec(wq, layer),
            _const_spec((1, d)),
            _row_spec(tm, LANES), _row_spec(tm, LANES), prev_spec, prev_spec,
            cache_spec, cache_spec,
            _layer_spec(wo, layer),
            _const_spec((1, d)),
            _const_spec((1, d)),
        ],
        out_specs=_row_spec(tm, d),
        out_shape=jax.ShapeDtypeStruct((rows, d), F32),
        scratch_shapes=[pltpu.VMEM((tm, d), F32), pltpu.VMEM((tm, d), BF16)],
        compiler_params=_params(),
        name="swa_attention",
    )(sinks, x, gpre, wq, bq, k, v, k, v, ck, cv, wo, bo, gpost)


def _mixing_weights(w_s, b_s, n):
    layers, groups = w_s.shape[:2]
    r = lax.broadcasted_iota(jnp.int32, (CHUNK, CHUNK), 0)
    c = lax.broadcasted_iota(jnp.int32, (CHUNK, CHUNK), 1)
    w_prompt = jnp.where(r >= c, w_s, 0.0)
    reps = CHUNK // n
    pick = (lax.broadcasted_iota(jnp.int32, (CHUNK, n), 0) % n
            == lax.broadcasted_iota(jnp.int32, (CHUNK, n), 1)).astype(F32)
    w_tiled = jnp.einsum('ra,lgab,cb->lgrc', pick, w_s[:, :, :n, :n], pick, precision=lax.Precision.HIGHEST)
    w_sample = jnp.where(jnp.logical_and(r >= c, r // n == c // n), w_tiled, 0.0)
    wmix = jnp.stack([w_prompt, w_sample], axis=1).astype(BF16)
    b_prompt = jnp.repeat(jnp.swapaxes(b_s, 1, 2), LANES, axis=2)
    b_sample = jnp.tile(b_prompt[:, :n], (1, reps, 1))
    bmix = jnp.stack([b_prompt, b_sample], axis=1)
    return wmix.reshape(layers * 2, groups, CHUNK, CHUNK), bmix.reshape(layers * 2, CHUNK, -1)


def kernel(x_prompt, x_sample, cache_k, cache_v, sg_norm_pre, sg_w_in, sg_ln_g, sg_ln_b, sg_w_s, sg_b_s, sg_w_out, sg_norm_post, kv_norm, w_kv, b_kv, sw_norm_pre, sw_w_q, sw_b_q, sw_sinks, sw_w_o, sw_b_o, sw_norm_post, f_norm_pre, f_w_gate, f_w_up, f_w_down, f_norm_post):
    bp, sp, d = x_prompt.shape
    bs, ss, _ = x_sample.shape
    n_a = sg_w_in.shape[0]
    n_b = sw_w_q.shape[0]
    depth = n_a + n_b
    n_p, n_s = bp * sp, bs * ss

    wmix, bmix = _mixing_weights(sg_w_s, sg_b_s, ss)
    row = lambda a: a.reshape(1, -1)
    kvh = cache_k.shape[2]
    ck = cache_k.reshape(bs, cache_k.shape[1], -1).astype(BF16)
    cv = cache_v.reshape(bs, cache_v.shape[1], -1).astype(BF16)

    w_in, w_out = sg_w_in.astype(BF16), sg_w_out.astype(BF16)
    w_q, w_o = sw_w_q.astype(BF16), sw_w_o.astype(BF16)
    w_gate, w_up, w_down = f_w_gate.astype(BF16), f_w_up.astype(BF16), f_w_down.astype(BF16)

    xs = [x_prompt.reshape(n_p, d), x_sample.reshape(n_s, d)]
    v_prompt_rows, v_sample_rows = [], []
    k_all = v_all = y_prompt = y_sample = None
    for layer in range(depth):
        if layer < n_a:
            l = layer
            x, vsp, vss = _mixer(
                xs, l, row(sg_norm_pre[l]), w_in, row(sg_ln_g[l]), row(sg_ln_b[l]), wmix, bmix, w_out,
                row(sg_norm_post[l]), n_prompt_rows=n_p, n_sample_rows=n_s, prompt_batch=bp)
            v_prompt_rows.append(vsp[:bp])
            v_sample_rows.append(vss.reshape(bs, ss, -1))
        else:
            l = layer - n_a
            x = _attn(x, l, k_all, v_all, ck, cv, sw_sinks[l], row(sw_norm_pre[l]), w_q, row(sw_b_q[l]), w_o,
                      row(sw_b_o[l]), row(sw_norm_post[l]), n_prompt_rows=n_p, prompt_batch=bp, dec_seq=ss)
        ffn_weights = (layer, row(f_norm_pre[layer]), w_gate, w_up, w_down, row(f_norm_post[layer]))
        if layer == n_a - 1:
            x, k_all, v_all = _ffn(x, *ffn_weights, kv_weights=(row(kv_norm), w_kv.astype(BF16), row(b_kv)))
        elif layer == depth - 1:
            y_prompt = _ffn(x, *ffn_weights, first_row=0, n_rows=n_p)
            y_sample = _ffn(x, *ffn_weights, first_row=n_p, n_rows=n_s)
        else:
            x = _ffn(x, *ffn_weights)
        xs = [x]

    def last_window(a):
        return a[:n_p].reshape(bp, sp, -1)[:, sp - CHUNK:].reshape(bp, CHUNK, kvh, -1)

    return (y_prompt.reshape(bp, sp, d), y_sample.reshape(bs, ss, d),
            jnp.stack(v_prompt_rows, axis=0), jnp.stack(v_sample_rows, axis=0),
            last_window(k_all), last_window(v_all),
            k_all[n_p:].reshape(bs, ss, kvh, -1), v_all[n_p:].reshape(bs, ss, kvh, -1))
```

```python
import functools

import jax
import jax.numpy as jnp
from jax import lax
from jax.experimental import pallas as pl
from jax.experimental.pallas import tpu as pltpu

F32 = jnp.float32
BF16 = jnp.bfloat16

RMS_EPS = 1e-6
LN_EPS = 1e-5
CHUNK = 128
LANES = 128
HEAD_DIM = 64
N_KV_HEADS = 2
HEADS_PER_KV = 8
PAIRS_PER_KV = HEADS_PER_KV // 2
ROW_TILE = 512
WIDE_TILE = 1024
SUB_TILE = 256
BF16_SUBLANES = 16
STAGE_BYTES = 768 * 1024
VMEM_LIMIT = 56 * 1024 * 1024
NEG_INF = float("-inf")


def _rms(x, g):
    ms = jnp.mean(x * x, axis=-1, keepdims=True)
    return x * lax.rsqrt(ms + RMS_EPS) * g


def _const_spec(shape):
    zeros = (0,) * len(shape)
    return pl.BlockSpec(shape, lambda i: zeros, pipeline_mode=pl.Buffered(1))


HBM_SPEC = pl.BlockSpec(memory_space=pl.ANY)


def _stage_rows(rows, cols):
    fits = [r for r in range(BF16_SUBLANES, rows + 1, BF16_SUBLANES)
            if rows % r == 0 and r * cols * 4 <= STAGE_BYTES]
    return max(fits)


def _weight_scratch(shape):
    rows, cols = shape
    return [pltpu.VMEM((rows, cols), BF16), pltpu.VMEM((2, _stage_rows(rows, cols), cols), F32)]


def _fetch_weight(w_hbm, w_vmem, stage, sem):
    rows = stage.shape[1]
    n_chunks = w_hbm.shape[0] // rows

    def chunk_copy(c, slot):
        return pltpu.make_async_copy(w_hbm.at[pl.ds(c * rows, rows), :], stage.at[slot], sem.at[slot])

    chunk_copy(0, 0).start()
    for c in range(n_chunks):
        slot = c % 2
        if c + 1 < n_chunks:
            chunk_copy(c + 1, 1 - slot).start()
        chunk_copy(c, slot).wait()
        w_vmem[pl.ds(c * rows, rows), :] = stage[slot].astype(BF16)


def _row_spec(tm, width, first_tile=0):
    return pl.BlockSpec((tm, width), lambda i: (i + first_tile, 0))


def _params():
    return pltpu.CompilerParams(dimension_semantics=("arbitrary",), vmem_limit_bytes=VMEM_LIMIT)


def _mixer_kernel(*refs, layer, n_prompt_tiles, split_input):
    n_x = 2 if split_input else 1
    gpre_ref, win_hbm, lng_ref, lnb_ref, wmix_ref, bmix_ref, wout_hbm, gpost_ref = refs[n_x:n_x + 8]
    xo_ref, vsp_ref, vss_ref, y_s, win_ref, win_stage, wout_ref, wout_stage, sem = refs[n_x + 8:]
    i = pl.program_id(0)
    tm, width = y_s.shape
    sub = SUB_TILE

    @pl.when(i == 0)
    def _():
        _fetch_weight(win_hbm.at[layer], win_ref, win_stage, sem)
        _fetch_weight(wout_hbm.at[layer], wout_ref, wout_stage, sem)

    def project_in(rows):
        if split_input:
            x = jnp.where(i < n_prompt_tiles, refs[0][rows, :], refs[1][rows, :])
        else:
            x = refs[0][rows, :]
        h = _rms(x, gpre_ref[...]).astype(BF16)
        return x, jnp.dot(h, win_ref[...], preferred_element_type=F32)

    def gate_and_norm(rows, z):
        z = 0.5 * z * (1.0 + lax.erf(z * (0.5 ** 0.5)))
        u = z[:, :width]
        v = z[:, width:]
        mu = jnp.mean(v, axis=-1, keepdims=True)
        vc = v - mu
        var = jnp.mean(vc * vc, axis=-1, keepdims=True)
        v = vc * lax.rsqrt(var + LN_EPS) * lng_ref[...] + lnb_ref[...]
        if rows.stop == tm:
            vsp_ref[...] = v[sub - CHUNK:, :]
        vss_ref[rows, :] = v
        return u, v.astype(BF16)

    def mix_and_project_out(rows, x, u, vb):
        bias = bmix_ref[...]
        for c in range(sub // CHUNK):
            r = slice(c * CHUNK, (c + 1) * CHUNK)
            yr = slice(rows.start + c * CHUNK, rows.start + (c + 1) * CHUNK)
            for g in range(width // LANES):
                cols = slice(g * LANES, (g + 1) * LANES)
                mixed = jnp.dot(wmix_ref[g], vb[r, cols], preferred_element_type=F32)
                y_s[yr, cols] = (u[r, cols] * (mixed + bias[:, cols])).astype(BF16)
        out = jnp.dot(y_s[rows, :], wout_ref[...], preferred_element_type=F32)
        xo_ref[rows, :] = x + _rms(out, gpost_ref[...])

    parts = [slice(s, s + sub) for s in range(0, tm, sub)]
    ahead = project_in(parts[0])
    for n, rows in enumerate(parts):
        x, z = ahead
        if n + 1 < len(parts):
            ahead = project_in(parts[n + 1])
        u, vb = gate_and_norm(rows, z)
        mix_and_project_out(rows, x, u, vb)


def _mixer(xs, layer, gpre, win, lng, lnb, wmix, bmix, wout, gpost, *, n_prompt_rows, n_sample_rows,
           prompt_batch):
    d = xs[0].shape[1]
    width = wout.shape[1]
    tm = WIDE_TILE
    npt = n_prompt_rows // tm
    tpb = npt // prompt_batch
    groups = wmix.shape[1]
    split_input = len(xs) == 2
    if split_input:
        x_specs = [pl.BlockSpec((tm, d), lambda i: (jnp.minimum(i, npt - 1), 0)),
                   pl.BlockSpec((tm, d), lambda i: (jnp.maximum(i - npt, 0), 0))]
    else:
        x_specs = [_row_spec(tm, d)]
    kern = functools.partial(_mixer_kernel, layer=layer, n_prompt_tiles=npt, split_input=split_input)
    return pl.pallas_call(
        kern,
        grid=((n_prompt_rows + n_sample_rows) // tm,),
        in_specs=x_specs + [
            _const_spec((1, d)),
            HBM_SPEC,
            _const_spec((1, width)),
            _const_spec((1, width)),
            pl.BlockSpec((None, groups, CHUNK, CHUNK), lambda i: (2 * layer + i // npt, 0, 0, 0)),
            pl.BlockSpec((None, CHUNK, width), lambda i: (2 * layer + i // npt, 0, 0)),
            HBM_SPEC,
            _const_spec((1, d)),
        ],
        out_specs=[
            _row_spec(tm, d),
            pl.BlockSpec((None, CHUNK, width), lambda i: (jnp.minimum(i // tpb, prompt_batch), 0, 0)),
            pl.BlockSpec((tm, width), lambda i: (jnp.maximum(i - npt, 0), 0)),
        ],
        out_shape=[
            jax.ShapeDtypeStruct((n_prompt_rows + n_sample_rows, d), F32),
            jax.ShapeDtypeStruct((prompt_batch + 1, CHUNK, width), F32),
            jax.ShapeDtypeStruct((n_sample_rows, width), F32),
        ],
        scratch_shapes=([pltpu.VMEM((tm, width), BF16)] + _weight_scratch(win.shape[1:])
                        + _weight_scratch(wout.shape[1:]) + [pltpu.SemaphoreType.DMA((2,))]),
        compiler_params=_params(),
        name="gmlp_mixer",
    )(*xs, gpre, win, lng, lnb, wmix, bmix, wout, gpost)


def _ffn_kernel(*refs, layer, with_kv, sample_first):
    x_ref, gpre_ref, wg_hbm, wu_hbm, wd_hbm, gpost_ref = refs[:6]
    wg_ref, up_stage, wu_ref, wd_ref, down_stage, sem = refs[-6:]
    rest = refs[6:-6]
    ys_ref = None
    if with_kv:
        kvg_ref, wkv_ref, bkv_ref, xo_ref, k_ref, v_ref = rest
    elif sample_first:
        xo_ref, ys_ref = rest
    else:
        (xo_ref,) = rest
    i = pl.program_id(0)
    tm = x_ref.shape[0]

    @pl.when(i == 0)
    def _():
        _fetch_weight(wg_hbm.at[layer], wg_ref, up_stage, sem)
        _fetch_weight(wu_hbm.at[layer], wu_ref, up_stage, sem)
        _fetch_weight(wd_hbm.at[layer], wd_ref, down_stage, sem)

    def project_up(rows):
        x = x_ref[rows, :]
        h = _rms(x, gpre_ref[...]).astype(BF16)
        g = jnp.dot(h, wg_ref[...], preferred_element_type=F32)
        u = jnp.dot(h, wu_ref[...], preferred_element_type=F32)
        return x, g, u

    def gate_and_project_down(rows, x, g, u):
        a = (g * (1.0 / (1.0 + jnp.exp(-g))) * u).astype(BF16)
        out = jnp.dot(a, wd_ref[...], preferred_element_type=F32)
        x_new = x + _rms(out, gpost_ref[...])
        xo_ref[rows, :] = x_new
        if with_kv:
            hk = _rms(x_new, kvg_ref[...]).astype(BF16)
            kv = jnp.dot(hk, wkv_ref[...], preferred_element_type=F32) + bkv_ref[...]
            half = k_ref.shape[1]
            k_ref[rows, :] = kv[:, :half]
            v_ref[rows, :] = kv[:, half:]

    parts = [slice(s, s + SUB_TILE) for s in range(0, tm, SUB_TILE)]
    ahead = project_up(parts[0])
    for n, rows in enumerate(parts):
        x, g, u = ahead
        if n + 1 < len(parts):
            ahead = project_up(parts[n + 1])
        gate_and_project_down(rows, x, g, u)

    if sample_first:
        @pl.when(i == 0)
        def _():
            ys_ref[...] = xo_ref[...]


def _ffn(x, layer, gpre, wg, wu, wd, gpost, kv_weights=None, n_prompt_rows=None):
    rows, d = x.shape
    tm = WIDE_TILE
    with_kv = kv_weights is not None
    sample_first = n_prompt_rows is not None
    weight_specs = [_const_spec((1, d)), HBM_SPEC, HBM_SPEC, HBM_SPEC, _const_spec((1, d))]
    args = [x, gpre, wg, wu, wd, gpost]
    if sample_first:
        npt = n_prompt_rows // tm
        assert rows - n_prompt_rows == tm, "the sample rows must fill exactly one tile"
        in_specs = [pl.BlockSpec((tm, d), lambda i: (jnp.where(i == 0, npt, i - 1), 0))] + weight_specs
        out_specs = [pl.BlockSpec((tm, d), lambda i: (jnp.maximum(i - 1, 0), 0)),
                     pl.BlockSpec((tm, d), lambda i: (0, 0))]
        out_shape = [jax.ShapeDtypeStruct((n_prompt_rows, d), F32), jax.ShapeDtypeStruct((tm, d), F32)]
    else:
        in_specs = [_row_spec(tm, d)] + weight_specs
        out_specs = [_row_spec(tm, d)]
        out_shape = [jax.ShapeDtypeStruct((rows, d), F32)]
    if with_kv:
        kvg, wkv, bkv = kv_weights
        half = wkv.shape[1] // 2
        in_specs += [_const_spec((1, d)), _const_spec((d, 2 * half)), _const_spec((1, 2 * half))]
        out_specs += [_row_spec(tm, half)] * 2
        out_shape += [jax.ShapeDtypeStruct((rows, half), F32)] * 2
        args += [kvg, wkv, bkv]
    scratch = (_weight_scratch(wg.shape[1:]) + _weight_scratch(wu.shape[1:])[:1] + _weight_scratch(wd.shape[1:])
               + [pltpu.SemaphoreType.DMA((2,))])
    out = pl.pallas_call(
        functools.partial(_ffn_kernel, layer=layer, with_kv=with_kv, sample_first=sample_first),
        grid=(rows // tm,),
        in_specs=in_specs,
        out_specs=out_specs,
        out_shape=out_shape,
        scratch_shapes=scratch,
        compiler_params=_params(),
        name="swiglu_ffn_kv" if with_kv else "swiglu_ffn",
    )(*args)
    return out if (with_kv or sample_first) else out[0]


def _split_heads(k):
    kr = pltpu.roll(k, HEAD_DIM, k.ndim - 1)
    low = lax.broadcasted_iota(jnp.int32, k.shape, k.ndim - 1) < HEAD_DIM
    zero = jnp.zeros_like(k)
    head0 = (jnp.where(low, k, zero).astype(BF16), jnp.where(low, zero, kr).astype(BF16))
    head1 = (jnp.where(low, kr, zero).astype(BF16), jnp.where(low, zero, k).astype(BF16))
    return head0, head1


def _split_heads_t(v):
    vt = v.T
    vt_swapped = jnp.concatenate([vt[HEAD_DIM:], vt[:HEAD_DIM]], axis=0)
    top = lax.broadcasted_iota(jnp.int32, vt.shape, 0) < HEAD_DIM
    zero = jnp.zeros_like(vt)
    head0 = (jnp.where(top, vt, zero).astype(BF16), jnp.where(top, zero, vt_swapped).astype(BF16))
    head1 = (jnp.where(top, vt_swapped, zero).astype(BF16), jnp.where(top, zero, vt).astype(BF16))
    return head0, head1


def _sink_column(sinks_ref, kh, parity, rows_per_pair, lead):
    pieces = []
    for j in range(PAIRS_PER_KV):
        s = sinks_ref[kh * HEADS_PER_KV + 2 * j + parity]
        pieces.append(jnp.full(lead + (rows_per_pair, 1), s, F32))
    return jnp.concatenate(pieces, axis=len(lead))


def _sink_row(sinks_ref, kh, parity):
    pieces = [jnp.full((1, CHUNK), sinks_ref[kh * HEADS_PER_KV + 2 * j + parity], F32)
              for j in range(PAIRS_PER_KV)]
    return jnp.concatenate(pieces, axis=1)


def _prompt_attention(i, q_s, att_s, sinks_ref, k_own, k_prev, v_own, v_prev, tiles_per_batch,
                      project_q, project_o):
    tm = q_s.shape[0]
    half_blocks = tm // CHUNK // 2
    project_q(slice(0, tm // 2))
    project_q(slice(tm // 2, tm))
    kv_lanes = PAIRS_PER_KV * LANES
    n_q = PAIRS_PER_KV * CHUNK
    k_own = _split_heads(k_own)
    k_prev = _split_heads(k_prev)
    v_own = _split_heads_t(v_own)
    v_prev = _split_heads_t(v_prev)
    slot = lax.broadcasted_iota(jnp.int32, (2 * CHUNK, n_q), 0) & (CHUNK - 1)
    t = lax.broadcasted_iota(jnp.int32, (2 * CHUNK, n_q), 1) & (CHUNK - 1)
    own = slot <= t
    no_prev = jnp.where((i % tiles_per_batch) == 0, NEG_INF, 0.0)
    top = lax.broadcasted_iota(jnp.int32, (LANES, n_q), 0) < HEAD_DIM

    def scores(blk, kh):
        rows = slice(blk * CHUNK, (blk + 1) * CHUNK)
        prow = slice((blk - 1) * CHUNK, blk * CHUNK)
        ka_o, kb_o = k_own[kh][0][rows], k_own[kh][1][rows]
        ka_p, kb_p = k_prev[kh] if blk == 0 else (k_own[kh][0][prow], k_own[kh][1][prow])
        keys = jnp.concatenate([ka_o, kb_o, ka_p, kb_p], axis=0)
        qs = jnp.concatenate(
            [q_s[rows, kh * kv_lanes + j * LANES: kh * kv_lanes + (j + 1) * LANES]
             for j in range(PAIRS_PER_KV)], axis=0).astype(BF16)
        s4 = lax.dot_general(keys, qs, (((1,), (1,)), ((), ())), preferred_element_type=F32)
        s_prev = s4[2 * CHUNK:]
        if blk == 0:
            s_prev = s_prev + no_prev
        return jnp.where(own, s4[:2 * CHUNK], s_prev)

    def finish(blk, kh, s):
        rows = slice(blk * CHUNK, (blk + 1) * CHUNK)
        prow = slice((blk - 1) * CHUNK, blk * CHUNK)
        va_o, vb_o = v_own[kh][0][:, rows], v_own[kh][1][:, rows]
        va_p, vb_p = v_prev[kh] if blk == 0 else (v_own[kh][0][:, prow], v_own[kh][1][:, prow])
        vals_t = jnp.concatenate([va_o, vb_o, va_p, vb_p], axis=1)
        sa, sb = s[:CHUNK], s[CHUNK:]
        sink_a = _sink_row(sinks_ref, kh, 0)
        sink_b = _sink_row(sinks_ref, kh, 1)
        ma = jnp.maximum(jnp.max(sa, axis=0, keepdims=True), sink_a)
        mb = jnp.maximum(jnp.max(sb, axis=0, keepdims=True), sink_b)
        pa = jnp.exp(sa - ma)
        pb = jnp.exp(sb - mb)
        inv_a = 1.0 / (jnp.sum(pa, axis=0, keepdims=True) + jnp.exp(sink_a - ma))
        inv_b = 1.0 / (jnp.sum(pb, axis=0, keepdims=True) + jnp.exp(sink_b - mb))
        p = jnp.concatenate([pa, pb], axis=0)
        p4 = jnp.concatenate([jnp.where(own, p, 0.0), jnp.where(own, 0.0, p)], axis=0).astype(BF16)
        o_t = jnp.dot(vals_t, p4, preferred_element_type=F32)
        o_t = o_t * jnp.where(top, inv_a, inv_b)
        for j in range(PAIRS_PER_KV):
            att_s[rows, kh * kv_lanes + j * LANES: kh * kv_lanes + (j + 1) * LANES] = (
                o_t[:, j * CHUNK:(j + 1) * CHUNK].T.astype(BF16))

    items = [(blk, kh) for blk in range(tm // CHUNK) for kh in range(N_KV_HEADS)]
    s_next = scores(*items[0])
    for n, item in enumerate(items):
        s_cur = s_next
        if n + 1 < len(items):
            s_next = scores(*items[n + 1])
        finish(*item, s_cur)
        blk, kh = item
        if kh == N_KV_HEADS - 1 and (blk + 1) % half_blocks == 0:
            first = (blk + 1 - half_blocks) * CHUNK
            project_o(slice(first, first + tm // 2))


def _sample_attention(q_s, att_s, sinks_ref, k_new, v_new, k_old, v_old, dec_seq):
    tm, d = q_s.shape
    kv_lanes = PAIRS_PER_KV * LANES
    nb = tm // dec_seq
    m_rows = PAIRS_PER_KV * dec_seq
    window = k_old.shape[1]
    k_new = _split_heads(k_new.reshape(nb, dec_seq, LANES))
    v_new = _split_heads(v_new.reshape(nb, dec_seq, LANES))
    k_old = _split_heads(k_old.astype(F32))
    v_old = _split_heads(v_old.astype(F32))
    t_c = lax.broadcasted_iota(jnp.int32, (nb, m_rows, 2 * window), 1) & (dec_seq - 1)
    c_c = lax.broadcasted_iota(jnp.int32, (nb, m_rows, 2 * window), 2) & (window - 1)
    valid_c = c_c > t_c
    t_n = lax.broadcasted_iota(jnp.int32, (nb, m_rows, 2 * dec_seq), 1) & (dec_seq - 1)
    col_n = lax.broadcasted_iota(jnp.int32, (nb, m_rows, 2 * dec_seq), 2)
    valid_n = (col_n & (dec_seq - 1)) <= t_n
    first_n = col_n < dec_seq
    lane_low = lax.broadcasted_iota(jnp.int32, (nb, m_rows, LANES), 2) < HEAD_DIM
    q3 = q_s[...].reshape(nb, dec_seq, d)
    bmm_nt = (((2,), (2,)), ((0,), (0,)))
    bmm_nn = (((2,), (1,)), ((0,), (0,)))
    for kh in range(N_KV_HEADS):
        rhs_c = jnp.concatenate([k_old[kh][0], k_old[kh][1]], axis=1)
        vv_c = jnp.concatenate([v_old[kh][0], v_old[kh][1]], axis=1)
        rhs_n = jnp.concatenate([k_new[kh][0].astype(F32), k_new[kh][1].astype(F32)], axis=1).astype(BF16)
        vv_n = jnp.concatenate([v_new[kh][0].astype(F32), v_new[kh][1].astype(F32)], axis=1).astype(BF16)
        qs = jnp.concatenate(
            [q3[:, :, kh * kv_lanes + j * LANES: kh * kv_lanes + (j + 1) * LANES]
             for j in range(PAIRS_PER_KV)], axis=1).astype(BF16)
        s_c = lax.dot_general(qs, rhs_c, bmm_nt, preferred_element_type=F32)
        s_n = lax.dot_general(qs, rhs_n, bmm_nt, preferred_element_type=F32)
        s_c = jnp.where(valid_c, s_c, NEG_INF)
        s_n = jnp.where(valid_n, s_n, NEG_INF)
        sa, sb = s_c[:, :, :window], s_c[:, :, window:]
        sink_a = _sink_column(sinks_ref, kh, 0, dec_seq, (1,))
        sink_b = _sink_column(sinks_ref, kh, 1, dec_seq, (1,))
        na = jnp.max(jnp.where(first_n, s_n, NEG_INF), axis=2, keepdims=True)
        nb_ = jnp.max(jnp.where(first_n, NEG_INF, s_n), axis=2, keepdims=True)
        ma = jnp.maximum(jnp.maximum(jnp.max(sa, axis=2, keepdims=True), na), sink_a)
        mb = jnp.maximum(jnp.maximum(jnp.max(sb, axis=2, keepdims=True), nb_), sink_b)
        pa = jnp.exp(sa - ma)
        pb = jnp.exp(sb - mb)
        pn = jnp.exp(s_n - jnp.where(first_n, ma, mb))
        den_a = (jnp.sum(pa, axis=2, keepdims=True) + jnp.exp(sink_a - ma)
                 + jnp.sum(jnp.where(first_n, pn, 0.0), axis=2, keepdims=True))
        den_b = (jnp.sum(pb, axis=2, keepdims=True) + jnp.exp(sink_b - mb)
                 + jnp.sum(jnp.where(first_n, 0.0, pn), axis=2, keepdims=True))
        p_c = jnp.concatenate([pa, pb], axis=2).astype(BF16)
        o = lax.dot_general(p_c, vv_c, bmm_nn, preferred_element_type=F32)
        o = o + lax.dot_general(pn.astype(BF16), vv_n, bmm_nn, preferred_element_type=F32)
        o = o * jnp.where(lane_low, 1.0 / den_a, 1.0 / den_b)
        for j in range(PAIRS_PER_KV):
            att_s[:, kh * kv_lanes + j * LANES: kh * kv_lanes + (j + 1) * LANES] = (
                o[:, j * dec_seq:(j + 1) * dec_seq, :].reshape(tm, LANES).astype(BF16))


def _attn_kernel(sinks_ref, x_ref, gpre_ref, wq_hbm, bq_ref, k_ref, v_ref, kprev_ref, vprev_ref,
                 ck_ref, cv_ref, wo_hbm, bo_ref, gpost_ref, xo_ref, q_s, att_s, wq_ref, stage, wo_ref, sem,
                 *, layer, n_prompt_tiles, tiles_per_batch, dec_seq):
    i = pl.program_id(0)
    tm = q_s.shape[0]

    @pl.when(i == 0)
    def _():
        _fetch_weight(wq_hbm.at[layer], wq_ref, stage, sem)
        _fetch_weight(wo_hbm.at[layer], wo_ref, stage, sem)

    def project_q(rows):
        h = _rms(x_ref[rows, :], gpre_ref[...]).astype(BF16)
        q = jnp.dot(h, wq_ref[...], preferred_element_type=F32) + bq_ref[...]
        q_s[rows, :] = q * (HEAD_DIM ** -0.5)

    def project_o(rows):
        out = jnp.dot(att_s[rows, :], wo_ref[...], preferred_element_type=F32) + bo_ref[...]
        xo_ref[rows, :] = x_ref[rows, :] + _rms(out, gpost_ref[...])

    @pl.when(i < n_prompt_tiles)
    def _():
        _prompt_attention(i, q_s, att_s, sinks_ref, k_ref[...], kprev_ref[...], v_ref[...], vprev_ref[...],
                          tiles_per_batch, project_q, project_o)

    @pl.when(i >= n_prompt_tiles)
    def _():
        project_q(slice(0, tm))
        _sample_attention(q_s, att_s, sinks_ref, k_ref[...], v_ref[...], ck_ref[...], cv_ref[...], dec_seq)
        project_o(slice(0, tm))


def _attn(x, layer, k, v, ck, cv, sinks, gpre, wq, bq, wo, bo, gpost, *, n_prompt_rows, prompt_batch, dec_seq):
    rows, d = x.shape
    tm = ROW_TILE
    npt = n_prompt_rows // tm
    tpb = npt // prompt_batch
    blocks_per_tile = tm // CHUNK
    nb = tm // dec_seq
    window = ck.shape[1]
    kern = functools.partial(_attn_kernel, layer=layer, n_prompt_tiles=npt, tiles_per_batch=tpb, dec_seq=dec_seq)
    prev_spec = pl.BlockSpec((CHUNK, LANES), lambda i: (jnp.maximum(i * blocks_per_tile - 1, 0), 0))
    cache_spec = pl.BlockSpec((nb, window, LANES), lambda i: (jnp.maximum(i - npt, 0), 0, 0))
    return pl.pallas_call(
        kern,
        grid=(rows // tm,),
        in_specs=[
            pl.BlockSpec(memory_space=pltpu.SMEM),
            _row_spec(tm, d),
            _const_spec((1, d)),
            HBM_SPEC,
            _const_spec((1, d)),
            _row_spec(tm, LANES), _row_spec(tm, LANES), prev_spec, prev_spec,
            cache_spec, cache_spec,
            HBM_SPEC,
            _const_spec((1, d)),
            _const_spec((1, d)),
        ],
        out_specs=_row_spec(tm, d),
        out_shape=jax.ShapeDtypeStruct((rows, d), F32),
        scratch_shapes=([pltpu.VMEM((tm, d), F32), pltpu.VMEM((tm, d), BF16)] + _weight_scratch(wq.shape[1:])
                        + _weight_scratch(wo.shape[1:])[:1] + [pltpu.SemaphoreType.DMA((2,))]),
        compiler_params=_params(),
        name="swa_attention",
    )(sinks, x, gpre, wq, bq, k, v, k, v, ck, cv, wo, bo, gpost)


def _mixing_weights(w_s, b_s, n):
    layers, groups = w_s.shape[:2]
    r = lax.broadcasted_iota(jnp.int32, (CHUNK, CHUNK), 0)
    c = lax.broadcasted_iota(jnp.int32, (CHUNK, CHUNK), 1)
    w_prompt = jnp.where(r >= c, w_s, 0.0)
    reps = CHUNK // n
    pick = (lax.broadcasted_iota(jnp.int32, (CHUNK, n), 0) % n
            == lax.broadcasted_iota(jnp.int32, (CHUNK, n), 1)).astype(F32)
    w_tiled = jnp.einsum('ra,lgab,cb->lgrc', pick, w_s[:, :, :n, :n], pick, precision=lax.Precision.HIGHEST)
    w_sample = jnp.where(jnp.logical_and(r >= c, r // n == c // n), w_tiled, 0.0)
    wmix = jnp.stack([w_prompt, w_sample], axis=1).astype(BF16)
    b_prompt = jnp.repeat(jnp.swapaxes(b_s, 1, 2), LANES, axis=2)
    b_sample = jnp.tile(b_prompt[:, :n], (1, reps, 1))
    bmix = jnp.stack([b_prompt, b_sample], axis=1)
    return wmix.reshape(layers * 2, groups, CHUNK, CHUNK), bmix.reshape(layers * 2, CHUNK, -1)


def kernel(x_prompt, x_sample, cache_k, cache_v, sg_norm_pre, sg_w_in, sg_ln_g, sg_ln_b, sg_w_s, sg_b_s, sg_w_out, sg_norm_post, kv_norm, w_kv, b_kv, sw_norm_pre, sw_w_q, sw_b_q, sw_sinks, sw_w_o, sw_b_o, sw_norm_post, f_norm_pre, f_w_gate, f_w_up, f_w_down, f_norm_post):
    bp, sp, d = x_prompt.shape
    bs, ss, _ = x_sample.shape
    n_a = sg_w_in.shape[0]
    n_b = sw_w_q.shape[0]
    depth = n_a + n_b
    n_p, n_s = bp * sp, bs * ss

    wmix, bmix = _mixing_weights(sg_w_s, sg_b_s, ss)
    row = lambda a: a.reshape(1, -1)
    kvh = cache_k.shape[2]
    ck = cache_k.reshape(bs, cache_k.shape[1], -1).astype(BF16)
    cv = cache_v.reshape(bs, cache_v.shape[1], -1).astype(BF16)

    w_in, w_out = sg_w_in, sg_w_out
    w_q, w_o = sw_w_q, sw_w_o
    w_gate, w_up, w_down = f_w_gate, f_w_up, f_w_down

    xs = [x_prompt.reshape(n_p, d), x_sample.reshape(n_s, d)]
    v_prompt_rows, v_sample_rows = [], []
    k_all = v_all = y_prompt = y_sample = None
    for layer in range(depth):
        if layer < n_a:
            l = layer
            x, vsp, vss = _mixer(
                xs, l, row(sg_norm_pre[l]), w_in, row(sg_ln_g[l]), row(sg_ln_b[l]), wmix, bmix, w_out,
                row(sg_norm_post[l]), n_prompt_rows=n_p, n_sample_rows=n_s, prompt_batch=bp)
            v_prompt_rows.append(vsp[:bp])
            v_sample_rows.append(vss.reshape(bs, ss, -1))
        else:
            l = layer - n_a
            x = _attn(x, l, k_all, v_all, ck, cv, sw_sinks[l], row(sw_norm_pre[l]), w_q, row(sw_b_q[l]), w_o,
                      row(sw_b_o[l]), row(sw_norm_post[l]), n_prompt_rows=n_p, prompt_batch=bp, dec_seq=ss)
        ffn_weights = (layer, row(f_norm_pre[layer]), w_gate, w_up, w_down, row(f_norm_post[layer]))
        if layer == n_a - 1:
            x, k_all, v_all = _ffn(x, *ffn_weights, kv_weights=(row(kv_norm), w_kv.astype(BF16), row(b_kv)))
        elif layer == depth - 1:
            y_prompt, y_sample = _ffn(x, *ffn_weights, n_prompt_rows=n_p)
        else:
            x = _ffn(x, *ffn_weights)
        xs = [x]

    def last_window(a):
        return a[:n_p].reshape(bp, sp, -1)[:, sp - CHUNK:].reshape(bp, CHUNK, kvh, -1)

    return (y_prompt.reshape(bp, sp, d), y_sample.reshape(bs, ss, d),
            jnp.stack(v_prompt_rows, axis=0), jnp.stack(v_sample_rows, axis=0),
            last_window(k_all), last_window(v_all),
            k_all[n_p:].reshape(bs, ss, kvh, -1), v_all[n_p:].reshape(bs, ss, kvh, -1))
```

```python
import functools

import jax
import jax.numpy as jnp
from jax import lax
from jax.experimental import pallas as pl
from jax.experimental.pallas import tpu as pltpu

F32 = jnp.float32
BF16 = jnp.bfloat16

RMS_EPS = 1e-6
LN_EPS = 1e-5
CHUNK = 128
LANES = 128
HEAD_DIM = 64
N_KV_HEADS = 2
HEADS_PER_KV = 8
PAIRS_PER_KV = HEADS_PER_KV // 2
ROW_TILE = 512
WIDE_TILE = 1024
SUB_TILE = 256
BF16_SUBLANES = 16
CONVERT_STEPS = 16
VMEM_LIMIT = 56 * 1024 * 1024
NEG_INF = float("-inf")


def _rms(x, g):
    ms = jnp.mean(x * x, axis=-1, keepdims=True)
    return x * lax.rsqrt(ms + RMS_EPS) * g


def _const_spec(shape):
    zeros = (0,) * len(shape)
    return pl.BlockSpec(shape, lambda i: zeros, pipeline_mode=pl.Buffered(1))


HBM_SPEC = pl.BlockSpec(memory_space=pl.ANY)


def _convert_pieces(sources, grid_steps):
    if not sources:
        return [], [], [], []
    assert grid_steps > CONVERT_STEPS, "the last chunk's write-back is waited on the step after it"
    n = len(sources)
    scratch = []
    for stack, _ in sources:
        rows, cols = stack.shape[1] // CONVERT_STEPS, stack.shape[2]
        assert rows * CONVERT_STEPS == stack.shape[1] and rows % BF16_SUBLANES == 0
        scratch += [pltpu.VMEM((rows, cols), F32), pltpu.VMEM((rows, cols), BF16)]
    scratch += [pltpu.SemaphoreType.DMA((n,)), pltpu.SemaphoreType.DMA((n,))]
    out_shape = [jax.ShapeDtypeStruct(stack.shape[1:], BF16) for stack, _ in sources]
    return [HBM_SPEC] * n, [HBM_SPEC] * n, out_shape, scratch


def _convert_ahead(i, layers, srcs, dsts, scratch):
    n = len(layers)
    if n == 0:
        return
    sem_read, sem_write = scratch[2 * n:]

    def read(k, c):
        stage = scratch[2 * k]
        rows = stage.shape[0]
        return pltpu.make_async_copy(srcs[k].at[layers[k], pl.ds(c * rows, rows), :], stage, sem_read.at[k])

    def write(k, c):
        stage = scratch[2 * k + 1]
        rows = stage.shape[0]
        return pltpu.make_async_copy(stage, dsts[k].at[pl.ds(c * rows, rows), :], sem_write.at[k])

    @pl.when(i == 0)
    def _():
        for k in range(n):
            read(k, 0).start()

    @pl.when(jnp.logical_and(i > 0, i <= CONVERT_STEPS))
    def _():
        for k in range(n):
            write(k, i - 1).wait()

    @pl.when(i < CONVERT_STEPS)
    def _():
        for k in range(n):
            read(k, i).wait()
            scratch[2 * k + 1][...] = scratch[2 * k][...].astype(BF16)
            write(k, i).start()

    @pl.when(i + 1 < CONVERT_STEPS)
    def _():
        for k in range(n):
            read(k, i + 1).start()


def _row_spec(tm, width):
    return pl.BlockSpec((tm, width), lambda i: (i, 0))


def _params():
    return pltpu.CompilerParams(dimension_semantics=("arbitrary",), vmem_limit_bytes=VMEM_LIMIT)


def _mixer_kernel(*refs, n_prompt_tiles, split_input, convert):
    n_x = 2 if split_input else 1
    n_c = len(convert)
    gpre_ref, win_ref, lng_ref, lnb_ref, wmix_ref, bmix_ref, wout_ref, gpost_ref = refs[n_x:n_x + 8]
    conv_srcs = refs[n_x + 8:n_x + 8 + n_c]
    xo_ref, vsp_ref, vss_ref = refs[n_x + 8 + n_c:n_x + 11 + n_c]
    conv_dsts = refs[n_x + 11 + n_c:n_x + 11 + 2 * n_c]
    y_s = refs[n_x + 11 + 2 * n_c]
    i = pl.program_id(0)
    tm, width = y_s.shape
    sub = SUB_TILE
    _convert_ahead(i, convert, conv_srcs, conv_dsts, refs[n_x + 12 + 2 * n_c:])

    def project_in(rows):
        if split_input:
            x = jnp.where(i < n_prompt_tiles, refs[0][rows, :], refs[1][rows, :])
        else:
            x = refs[0][rows, :]
        h = _rms(x, gpre_ref[...]).astype(BF16)
        return x, jnp.dot(h, win_ref[...], preferred_element_type=F32)

    def gate_and_norm(rows, z):
        z = 0.5 * z * (1.0 + lax.erf(z * (0.5 ** 0.5)))
        u = z[:, :width]
        v = z[:, width:]
        mu = jnp.mean(v, axis=-1, keepdims=True)
        vc = v - mu
        var = jnp.mean(vc * vc, axis=-1, keepdims=True)
        v = vc * lax.rsqrt(var + LN_EPS) * lng_ref[...] + lnb_ref[...]
        if rows.stop == tm:
            vsp_ref[...] = v[sub - CHUNK:, :]
        vss_ref[rows, :] = v
        return u, v.astype(BF16)

    def mix_and_project_out(rows, x, u, vb):
        bias = bmix_ref[...]
        for c in range(sub // CHUNK):
            r = slice(c * CHUNK, (c + 1) * CHUNK)
            yr = slice(rows.start + c * CHUNK, rows.start + (c + 1) * CHUNK)
            for g in range(width // LANES):
                cols = slice(g * LANES, (g + 1) * LANES)
                mixed = jnp.dot(wmix_ref[g], vb[r, cols], preferred_element_type=F32)
                y_s[yr, cols] = (u[r, cols] * (mixed + bias[:, cols])).astype(BF16)
        out = jnp.dot(y_s[rows, :], wout_ref[...], preferred_element_type=F32)
        xo_ref[rows, :] = x + _rms(out, gpost_ref[...])

    parts = [slice(s, s + sub) for s in range(0, tm, sub)]
    ahead = project_in(parts[0])
    for n, rows in enumerate(parts):
        x, z = ahead
        if n + 1 < len(parts):
            ahead = project_in(parts[n + 1])
        u, vb = gate_and_norm(rows, z)
        mix_and_project_out(rows, x, u, vb)


def _mixer(xs, layer, gpre, win, lng, lnb, wmix, bmix, wout, gpost, *, n_prompt_rows, n_sample_rows,
           prompt_batch, convert):
    d = xs[0].shape[1]
    width = wout.shape[0]
    tm = WIDE_TILE
    npt = n_prompt_rows // tm
    tpb = npt // prompt_batch
    groups = wmix.shape[1]
    split_input = len(xs) == 2
    if split_input:
        x_specs = [pl.BlockSpec((tm, d), lambda i: (jnp.minimum(i, npt - 1), 0)),
                   pl.BlockSpec((tm, d), lambda i: (jnp.maximum(i - npt, 0), 0))]
    else:
        x_specs = [_row_spec(tm, d)]
    grid_steps = (n_prompt_rows + n_sample_rows) // tm
    conv_in, conv_out, conv_shape, conv_scratch = _convert_pieces(convert, grid_steps)
    kern = functools.partial(_mixer_kernel, n_prompt_tiles=npt, split_input=split_input,
                             convert=tuple(l for _, l in convert))
    out = pl.pallas_call(
        kern,
        grid=(grid_steps,),
        in_specs=x_specs + [
            _const_spec((1, d)),
            _const_spec(win.shape),
            _const_spec((1, width)),
            _const_spec((1, width)),
            pl.BlockSpec((None, groups, CHUNK, CHUNK), lambda i: (2 * layer + i // npt, 0, 0, 0)),
            pl.BlockSpec((None, CHUNK, width), lambda i: (2 * layer + i // npt, 0, 0)),
            _const_spec(wout.shape),
            _const_spec((1, d)),
        ] + conv_in,
        out_specs=[
            _row_spec(tm, d),
            pl.BlockSpec((None, CHUNK, width), lambda i: (jnp.minimum(i // tpb, prompt_batch), 0, 0)),
            pl.BlockSpec((tm, width), lambda i: (jnp.maximum(i - npt, 0), 0)),
        ] + conv_out,
        out_shape=[
            jax.ShapeDtypeStruct((n_prompt_rows + n_sample_rows, d), F32),
            jax.ShapeDtypeStruct((prompt_batch + 1, CHUNK, width), F32),
            jax.ShapeDtypeStruct((n_sample_rows, width), F32),
        ] + conv_shape,
        scratch_shapes=[pltpu.VMEM((tm, width), BF16)] + conv_scratch,
        compiler_params=_params(),
        name="gmlp_mixer",
    )(*xs, gpre, win, lng, lnb, wmix, bmix, wout, gpost, *[stack for stack, _ in convert])
    return out[:3], out[3:]


def _ffn_kernel(*refs, with_kv, sample_first, convert):
    x_ref, gpre_ref, wg_ref, wu_ref, wd_ref, gpost_ref = refs[:6]
    n_in = 9 if with_kv else 6
    n_out = 3 if with_kv else (2 if sample_first else 1)
    n_c = len(convert)
    conv_srcs = refs[n_in:n_in + n_c]
    outs = refs[n_in + n_c:n_in + n_c + n_out]
    conv_dsts = refs[n_in + n_c + n_out:n_in + 2 * n_c + n_out]
    ys_ref = None
    if with_kv:
        kvg_ref, wkv_ref, bkv_ref = refs[6:9]
        xo_ref, k_ref, v_ref = outs
    elif sample_first:
        xo_ref, ys_ref = outs
    else:
        (xo_ref,) = outs
    i = pl.program_id(0)
    tm = x_ref.shape[0]
    _convert_ahead(i, convert, conv_srcs, conv_dsts, refs[n_in + 2 * n_c + n_out:])

    def project_up(rows):
        x = x_ref[rows, :]
        h = _rms(x, gpre_ref[...]).astype(BF16)
        g = jnp.dot(h, wg_ref[...], preferred_element_type=F32)
        u = jnp.dot(h, wu_ref[...], preferred_element_type=F32)
        return x, g, u

    def gate_and_project_down(rows, x, g, u):
        a = (g * (1.0 / (1.0 + jnp.exp(-g))) * u).astype(BF16)
        out = jnp.dot(a, wd_ref[...], preferred_element_type=F32)
        x_new = x + _rms(out, gpost_ref[...])
        xo_ref[rows, :] = x_new
        if with_kv:
            hk = _rms(x_new, kvg_ref[...]).astype(BF16)
            kv = jnp.dot(hk, wkv_ref[...], preferred_element_type=F32) + bkv_ref[...]
            half = k_ref.shape[1]
            k_ref[rows, :] = kv[:, :half]
            v_ref[rows, :] = kv[:, half:]

    parts = [slice(s, s + SUB_TILE) for s in range(0, tm, SUB_TILE)]
    ahead = project_up(parts[0])
    for n, rows in enumerate(parts):
        x, g, u = ahead
        if n + 1 < len(parts):
            ahead = project_up(parts[n + 1])
        gate_and_project_down(rows, x, g, u)

    if sample_first:
        @pl.when(i == 0)
        def _():
            ys_ref[...] = xo_ref[...]


def _ffn(x, gpre, wg, wu, wd, gpost, kv_weights=None, n_prompt_rows=None, convert=()):
    rows, d = x.shape
    tm = WIDE_TILE
    with_kv = kv_weights is not None
    sample_first = n_prompt_rows is not None
    weight_specs = [_const_spec((1, d)), _const_spec(wg.shape), _const_spec(wu.shape), _const_spec(wd.shape),
                    _const_spec((1, d))]
    args = [x, gpre, wg, wu, wd, gpost]
    if sample_first:
        npt = n_prompt_rows // tm
        assert rows - n_prompt_rows == tm, "the sample rows must fill exactly one tile"
        in_specs = [pl.BlockSpec((tm, d), lambda i: (jnp.where(i == 0, npt, i - 1), 0))] + weight_specs
        out_specs = [pl.BlockSpec((tm, d), lambda i: (jnp.maximum(i - 1, 0), 0)),
                     pl.BlockSpec((tm, d), lambda i: (0, 0))]
        out_shape = [jax.ShapeDtypeStruct((n_prompt_rows, d), F32), jax.ShapeDtypeStruct((tm, d), F32)]
    else:
        in_specs = [_row_spec(tm, d)] + weight_specs
        out_specs = [_row_spec(tm, d)]
        out_shape = [jax.ShapeDtypeStruct((rows, d), F32)]
    if with_kv:
        kvg, wkv, bkv = kv_weights
        half = wkv.shape[1] // 2
        in_specs += [_const_spec((1, d)), _const_spec((d, 2 * half)), _const_spec((1, 2 * half))]
        out_specs += [_row_spec(tm, half)] * 2
        out_shape += [jax.ShapeDtypeStruct((rows, half), F32)] * 2
        args += [kvg, wkv, bkv]
    conv_in, conv_out, conv_shape, conv_scratch = _convert_pieces(convert, rows // tm)
    n_out = len(out_shape)
    out = pl.pallas_call(
        functools.partial(_ffn_kernel, with_kv=with_kv, sample_first=sample_first,
                          convert=tuple(l for _, l in convert)),
        grid=(rows // tm,),
        in_specs=in_specs + conv_in,
        out_specs=out_specs + conv_out,
        out_shape=out_shape + conv_shape,
        scratch_shapes=conv_scratch,
        compiler_params=_params(),
        name="swiglu_ffn_kv" if with_kv else "swiglu_ffn",
    )(*args, *[stack for stack, _ in convert])
    return out[:n_out], out[n_out:]


def _split_heads(k):
    kr = pltpu.roll(k, HEAD_DIM, k.ndim - 1)
    low = lax.broadcasted_iota(jnp.int32, k.shape, k.ndim - 1) < HEAD_DIM
    zero = jnp.zeros_like(k)
    head0 = (jnp.where(low, k, zero).astype(BF16), jnp.where(low, zero, kr).astype(BF16))
    head1 = (jnp.where(low, kr, zero).astype(BF16), jnp.where(low, zero, k).astype(BF16))
    return head0, head1


def _split_heads_t(v):
    vt = v.T
    vt_swapped = jnp.concatenate([vt[HEAD_DIM:], vt[:HEAD_DIM]], axis=0)
    top = lax.broadcasted_iota(jnp.int32, vt.shape, 0) < HEAD_DIM
    zero = jnp.zeros_like(vt)
    head0 = (jnp.where(top, vt, zero).astype(BF16), jnp.where(top, zero, vt_swapped).astype(BF16))
    head1 = (jnp.where(top, vt_swapped, zero).astype(BF16), jnp.where(top, zero, vt).astype(BF16))
    return head0, head1


def _sink_column(sinks_ref, kh, parity, rows_per_pair, lead):
    pieces = []
    for j in range(PAIRS_PER_KV):
        s = sinks_ref[kh * HEADS_PER_KV + 2 * j + parity]
        pieces.append(jnp.full(lead + (rows_per_pair, 1), s, F32))
    return jnp.concatenate(pieces, axis=len(lead))


def _sink_row(sinks_ref, kh, parity):
    pieces = [jnp.full((1, CHUNK), sinks_ref[kh * HEADS_PER_KV + 2 * j + parity], F32)
              for j in range(PAIRS_PER_KV)]
    return jnp.concatenate(pieces, axis=1)


def _prompt_attention(i, q_s, att_s, sinks_ref, k_own, k_prev, v_own, v_prev, tiles_per_batch,
                      project_q, project_o):
    tm = q_s.shape[0]
    half_blocks = tm // CHUNK // 2
    project_q(slice(0, tm // 2))
    project_q(slice(tm // 2, tm))
    kv_lanes = PAIRS_PER_KV * LANES
    n_q = PAIRS_PER_KV * CHUNK
    k_own = _split_heads(k_own)
    k_prev = _split_heads(k_prev)
    v_own = _split_heads_t(v_own)
    v_prev = _split_heads_t(v_prev)
    slot = lax.broadcasted_iota(jnp.int32, (2 * CHUNK, n_q), 0) & (CHUNK - 1)
    t = lax.broadcasted_iota(jnp.int32, (2 * CHUNK, n_q), 1) & (CHUNK - 1)
    own = slot <= t
    no_prev = jnp.where((i % tiles_per_batch) == 0, NEG_INF, 0.0)
    top = lax.broadcasted_iota(jnp.int32, (LANES, n_q), 0) < HEAD_DIM

    def scores(blk, kh):
        rows = slice(blk * CHUNK, (blk + 1) * CHUNK)
        prow = slice((blk - 1) * CHUNK, blk * CHUNK)
        ka_o, kb_o = k_own[kh][0][rows], k_own[kh][1][rows]
        ka_p, kb_p = k_prev[kh] if blk == 0 else (k_own[kh][0][prow], k_own[kh][1][prow])
        keys = jnp.concatenate([ka_o, kb_o, ka_p, kb_p], axis=0)
        qs = jnp.concatenate(
            [q_s[rows, kh * kv_lanes + j * LANES: kh * kv_lanes + (j + 1) * LANES]
             for j in range(PAIRS_PER_KV)], axis=0).astype(BF16)
        s4 = lax.dot_general(keys, qs, (((1,), (1,)), ((), ())), preferred_element_type=F32)
        s_prev = s4[2 * CHUNK:]
        if blk == 0:
            s_prev = s_prev + no_prev
        return jnp.where(own, s4[:2 * CHUNK], s_prev)

    def finish(blk, kh, s):
        rows = slice(blk * CHUNK, (blk + 1) * CHUNK)
        prow = slice((blk - 1) * CHUNK, blk * CHUNK)
        va_o, vb_o = v_own[kh][0][:, rows], v_own[kh][1][:, rows]
        va_p, vb_p = v_prev[kh] if blk == 0 else (v_own[kh][0][:, prow], v_own[kh][1][:, prow])
        vals_t = jnp.concatenate([va_o, vb_o, va_p, vb_p], axis=1)
        sa, sb = s[:CHUNK], s[CHUNK:]
        sink_a = _sink_row(sinks_ref, kh, 0)
        sink_b = _sink_row(sinks_ref, kh, 1)
        ma = jnp.maximum(jnp.max(sa, axis=0, keepdims=True), sink_a)
        mb = jnp.maximum(jnp.max(sb, axis=0, keepdims=True), sink_b)
        pa = jnp.exp(sa - ma)
        pb = jnp.exp(sb - mb)
        inv_a = 1.0 / (jnp.sum(pa, axis=0, keepdims=True) + jnp.exp(sink_a - ma))
        inv_b = 1.0 / (jnp.sum(pb, axis=0, keepdims=True) + jnp.exp(sink_b - mb))
        p = jnp.concatenate([pa, pb], axis=0)
        p4 = jnp.concatenate([jnp.where(own, p, 0.0), jnp.where(own, 0.0, p)], axis=0).astype(BF16)
        o_t = jnp.dot(vals_t, p4, preferred_element_type=F32)
        o_t = o_t * jnp.where(top, inv_a, inv_b)
        for j in range(PAIRS_PER_KV):
            att_s[rows, kh * kv_lanes + j * LANES: kh * kv_lanes + (j + 1) * LANES] = (
                o_t[:, j * CHUNK:(j + 1) * CHUNK].T.astype(BF16))

    items = [(blk, kh) for blk in range(tm // CHUNK) for kh in range(N_KV_HEADS)]
    s_next = scores(*items[0])
    for n, item in enumerate(items):
        s_cur = s_next
        if n + 1 < len(items):
            s_next = scores(*items[n + 1])
        finish(*item, s_cur)
        blk, kh = item
        if kh == N_KV_HEADS - 1 and (blk + 1) % half_blocks == 0:
            first = (blk + 1 - half_blocks) * CHUNK
            project_o(slice(first, first + tm // 2))


def _sample_attention(q_s, att_s, sinks_ref, k_new, v_new, k_old, v_old, dec_seq):
    tm, d = q_s.shape
    kv_lanes = PAIRS_PER_KV * LANES
    nb = tm // dec_seq
    m_rows = PAIRS_PER_KV * dec_seq
    window = k_old.shape[1]
    k_new = _split_heads(k_new.reshape(nb, dec_seq, LANES))
    v_new = _split_heads(v_new.reshape(nb, dec_seq, LANES))
    k_old = _split_heads(k_old.astype(F32))
    v_old = _split_heads(v_old.astype(F32))
    t_c = lax.broadcasted_iota(jnp.int32, (nb, m_rows, 2 * window), 1) & (dec_seq - 1)
    c_c = lax.broadcasted_iota(jnp.int32, (nb, m_rows, 2 * window), 2) & (window - 1)
    valid_c = c_c > t_c
    t_n = lax.broadcasted_iota(jnp.int32, (nb, m_rows, 2 * dec_seq), 1) & (dec_seq - 1)
    col_n = lax.broadcasted_iota(jnp.int32, (nb, m_rows, 2 * dec_seq), 2)
    valid_n = (col_n & (dec_seq - 1)) <= t_n
    first_n = col_n < dec_seq
    lane_low = lax.broadcasted_iota(jnp.int32, (nb, m_rows, LANES), 2) < HEAD_DIM
    q3 = q_s[...].reshape(nb, dec_seq, d)
    bmm_nt = (((2,), (2,)), ((0,), (0,)))
    bmm_nn = (((2,), (1,)), ((0,), (0,)))
    for kh in range(N_KV_HEADS):
        rhs_c = jnp.concatenate([k_old[kh][0], k_old[kh][1]], axis=1)
        vv_c = jnp.concatenate([v_old[kh][0], v_old[kh][1]], axis=1)
        rhs_n = jnp.concatenate([k_new[kh][0].astype(F32), k_new[kh][1].astype(F32)], axis=1).astype(BF16)
        vv_n = jnp.concatenate([v_new[kh][0].astype(F32), v_new[kh][1].astype(F32)], axis=1).astype(BF16)
        qs = jnp.concatenate(
            [q3[:, :, kh * kv_lanes + j * LANES: kh * kv_lanes + (j + 1) * LANES]
             for j in range(PAIRS_PER_KV)], axis=1).astype(BF16)
        s_c = lax.dot_general(qs, rhs_c, bmm_nt, preferred_element_type=F32)
        s_n = lax.dot_general(qs, rhs_n, bmm_nt, preferred_element_type=F32)
        s_c = jnp.where(valid_c, s_c, NEG_INF)
        s_n = jnp.where(valid_n, s_n, NEG_INF)
        sa, sb = s_c[:, :, :window], s_c[:, :, window:]
        sink_a = _sink_column(sinks_ref, kh, 0, dec_seq, (1,))
        sink_b = _sink_column(sinks_ref, kh, 1, dec_seq, (1,))
        na = jnp.max(jnp.where(first_n, s_n, NEG_INF), axis=2, keepdims=True)
        nb_ = jnp.max(jnp.where(first_n, NEG_INF, s_n), axis=2, keepdims=True)
        ma = jnp.maximum(jnp.maximum(jnp.max(sa, axis=2, keepdims=True), na), sink_a)
        mb = jnp.maximum(jnp.maximum(jnp.max(sb, axis=2, keepdims=True), nb_), sink_b)
        pa = jnp.exp(sa - ma)
        pb = jnp.exp(sb - mb)
        pn = jnp.exp(s_n - jnp.where(first_n, ma, mb))
        den_a = (jnp.sum(pa, axis=2, keepdims=True) + jnp.exp(sink_a - ma)
                 + jnp.sum(jnp.where(first_n, pn, 0.0), axis=2, keepdims=True))
        den_b = (jnp.sum(pb, axis=2, keepdims=True) + jnp.exp(sink_b - mb)
                 + jnp.sum(jnp.where(first_n, 0.0, pn), axis=2, keepdims=True))
        p_c = jnp.concatenate([pa, pb], axis=2).astype(BF16)
        o = lax.dot_general(p_c, vv_c, bmm_nn, preferred_element_type=F32)
        o = o + lax.dot_general(pn.astype(BF16), vv_n, bmm_nn, preferred_element_type=F32)
        o = o * jnp.where(lane_low, 1.0 / den_a, 1.0 / den_b)
        for j in range(PAIRS_PER_KV):
            att_s[:, kh * kv_lanes + j * LANES: kh * kv_lanes + (j + 1) * LANES] = (
                o[:, j * dec_seq:(j + 1) * dec_seq, :].reshape(tm, LANES).astype(BF16))


def _attn_kernel(*refs, n_prompt_tiles, tiles_per_batch, dec_seq, convert):
    (sinks_ref, x_ref, gpre_ref, wq_ref, bq_ref, k_ref, v_ref, kprev_ref, vprev_ref,
     ck_ref, cv_ref, wo_ref, bo_ref, gpost_ref) = refs[:14]
    n_c = len(convert)
    conv_srcs = refs[14:14 + n_c]
    xo_ref = refs[14 + n_c]
    conv_dsts = refs[15 + n_c:15 + 2 * n_c]
    q_s, att_s = refs[15 + 2 * n_c:17 + 2 * n_c]
    i = pl.program_id(0)
    tm = q_s.shape[0]
    _convert_ahead(i, convert, conv_srcs, conv_dsts, refs[17 + 2 * n_c:])

    def project_q(rows):
        h = _rms(x_ref[rows, :], gpre_ref[...]).astype(BF16)
        q = jnp.dot(h, wq_ref[...], preferred_element_type=F32) + bq_ref[...]
        q_s[rows, :] = q * (HEAD_DIM ** -0.5)

    def project_o(rows):
        out = jnp.dot(att_s[rows, :], wo_ref[...], preferred_element_type=F32) + bo_ref[...]
        xo_ref[rows, :] = x_ref[rows, :] + _rms(out, gpost_ref[...])

    @pl.when(i < n_prompt_tiles)
    def _():
        _prompt_attention(i, q_s, att_s, sinks_ref, k_ref[...], kprev_ref[...], v_ref[...], vprev_ref[...],
                          tiles_per_batch, project_q, project_o)

    @pl.when(i >= n_prompt_tiles)
    def _():
        project_q(slice(0, tm))
        _sample_attention(q_s, att_s, sinks_ref, k_ref[...], v_ref[...], ck_ref[...], cv_ref[...], dec_seq)
        project_o(slice(0, tm))


def _attn(x, k, v, ck, cv, sinks, gpre, wq, bq, wo, bo, gpost, *, n_prompt_rows, prompt_batch, dec_seq,
          convert):
    rows, d = x.shape
    tm = ROW_TILE
    npt = n_prompt_rows // tm
    tpb = npt // prompt_batch
    blocks_per_tile = tm // CHUNK
    nb = tm // dec_seq
    window = ck.shape[1]
    conv_in, conv_out, conv_shape, conv_scratch = _convert_pieces(convert, rows // tm)
    kern = functools.partial(_attn_kernel, n_prompt_tiles=npt, tiles_per_batch=tpb, dec_seq=dec_seq,
                             convert=tuple(l for _, l in convert))
    prev_spec = pl.BlockSpec((CHUNK, LANES), lambda i: (jnp.maximum(i * blocks_per_tile - 1, 0), 0))
    cache_spec = pl.BlockSpec((nb, window, LANES), lambda i: (jnp.maximum(i - npt, 0), 0, 0))
    out = pl.pallas_call(
        kern,
        grid=(rows // tm,),
        in_specs=[
            pl.BlockSpec(memory_space=pltpu.SMEM),
            _row_spec(tm, d),
            _const_spec((1, d)),
            _const_spec(wq.shape),
            _const_spec((1, d)),
            _row_spec(tm, LANES), _row_spec(tm, LANES), prev_spec, prev_spec,
            cache_spec, cache_spec,
            _const_spec(wo.shape),
            _const_spec((1, d)),
            _const_spec((1, d)),
        ] + conv_in,
        out_specs=[_row_spec(tm, d)] + conv_out,
        out_shape=[jax.ShapeDtypeStruct((rows, d), F32)] + conv_shape,
        scratch_shapes=[pltpu.VMEM((tm, d), F32), pltpu.VMEM((tm, d), BF16)] + conv_scratch,
        compiler_params=_params(),
        name="swa_attention",
    )(sinks, x, gpre, wq, bq, k, v, k, v, ck, cv, wo, bo, gpost, *[stack for stack, _ in convert])
    return out[0], out[1:]


def _mixing_weights(w_s, b_s, n):
    layers, groups = w_s.shape[:2]
    r = lax.broadcasted_iota(jnp.int32, (CHUNK, CHUNK), 0)
    c = lax.broadcasted_iota(jnp.int32, (CHUNK, CHUNK), 1)
    w_prompt = jnp.where(r >= c, w_s, 0.0)
    reps = CHUNK // n
    pick = (lax.broadcasted_iota(jnp.int32, (CHUNK, n), 0) % n
            == lax.broadcasted_iota(jnp.int32, (CHUNK, n), 1)).astype(F32)
    w_tiled = jnp.einsum('ra,lgab,cb->lgrc', pick, w_s[:, :, :n, :n], pick, precision=lax.Precision.HIGHEST)
    w_sample = jnp.where(jnp.logical_and(r >= c, r // n == c // n), w_tiled, 0.0)
    wmix = jnp.stack([w_prompt, w_sample], axis=1).astype(BF16)
    b_prompt = jnp.repeat(jnp.swapaxes(b_s, 1, 2), LANES, axis=2)
    b_sample = jnp.tile(b_prompt[:, :n], (1, reps, 1))
    bmix = jnp.stack([b_prompt, b_sample], axis=1)
    return wmix.reshape(layers * 2, groups, CHUNK, CHUNK), bmix.reshape(layers * 2, CHUNK, -1)


def kernel(x_prompt, x_sample, cache_k, cache_v, sg_norm_pre, sg_w_in, sg_ln_g, sg_ln_b, sg_w_s, sg_b_s, sg_w_out, sg_norm_post, kv_norm, w_kv, b_kv, sw_norm_pre, sw_w_q, sw_b_q, sw_sinks, sw_w_o, sw_b_o, sw_norm_post, f_norm_pre, f_w_gate, f_w_up, f_w_down, f_norm_post):
    bp, sp, d = x_prompt.shape
    bs, ss, _ = x_sample.shape
    n_a = sg_w_in.shape[0]
    n_b = sw_w_q.shape[0]
    depth = n_a + n_b
    n_p, n_s = bp * sp, bs * ss

    wmix, bmix = _mixing_weights(sg_w_s, sg_b_s, ss)
    row = lambda a: a.reshape(1, -1)
    kvh = cache_k.shape[2]
    ck = cache_k.reshape(bs, cache_k.shape[1], -1).astype(BF16)
    cv = cache_v.reshape(bs, cache_v.shape[1], -1).astype(BF16)

    def mixer_sources(l):
        return [(sg_w_in, l), (sg_w_out, l)]

    def attn_sources(l):
        return [(sw_w_q, l), (sw_w_o, l)]

    def ffn_sources(layer):
        return [(f_w_gate, layer), (f_w_up, layer), (f_w_down, layer)]

    def sources_after(layer, in_ffn):
        if not in_ffn:
            return ffn_sources(layer)
        if layer + 1 == depth:
            return []
        return mixer_sources(layer + 1) if layer + 1 < n_a else attn_sources(layer + 1 - n_a)

    xs = [x_prompt.reshape(n_p, d), x_sample.reshape(n_s, d)]
    v_prompt_rows, v_sample_rows = [], []
    k_all = v_all = y_prompt = y_sample = None
    weights = [stack[l].astype(BF16) for stack, l in (mixer_sources(0) if n_a else attn_sources(0))]
    for layer in range(depth):
        if layer < n_a:
            l = layer
            (x, vsp, vss), weights = _mixer(
                xs, l, row(sg_norm_pre[l]), weights[0], row(sg_ln_g[l]), row(sg_ln_b[l]), wmix, bmix, weights[1],
                row(sg_norm_post[l]), n_prompt_rows=n_p, n_sample_rows=n_s, prompt_batch=bp,
                convert=sources_after(layer, False))
            v_prompt_rows.append(vsp[:bp])
            v_sample_rows.append(vss.reshape(bs, ss, -1))
        else:
            l = layer - n_a
            x, weights = _attn(x, k_all, v_all, ck, cv, sw_sinks[l], row(sw_norm_pre[l]), weights[0],
                               row(sw_b_q[l]), weights[1], row(sw_b_o[l]), row(sw_norm_post[l]),
                               n_prompt_rows=n_p, prompt_batch=bp, dec_seq=ss, convert=sources_after(layer, False))
        ffn_args = (row(f_norm_pre[layer]), *weights, row(f_norm_post[layer]))
        nxt = sources_after(layer, True)
        if layer == n_a - 1:
            (x, k_all, v_all), weights = _ffn(
                x, *ffn_args, kv_weights=(row(kv_norm), w_kv.astype(BF16), row(b_kv)), convert=nxt)
        elif layer == depth - 1:
            (y_prompt, y_sample), weights = _ffn(x, *ffn_args, n_prompt_rows=n_p, convert=nxt)
        else:
            (x,), weights = _ffn(x, *ffn_args, convert=nxt)
        xs = [x]

    def last_window(a):
        return a[:n_p].reshape(bp, sp, -1)[:, sp - CHUNK:].reshape(bp, CHUNK, kvh, -1)

    return (y_prompt.reshape(bp, sp, d), y_sample.reshape(bs, ss, d),
            jnp.stack(v_prompt_rows, axis=0), jnp.stack(v_sample_rows, axis=0),
            last_window(k_all), last_window(v_all),
            k_all[n_p:].reshape(bs, ss, kvh, -1), v_all[n_p:].reshape(bs, ss, kvh, -1))
```

```python
import functools

import jax
import jax.numpy as jnp
from jax import lax
from jax.experimental import pallas as pl
from jax.experimental.pallas import tpu as pltpu

F32 = jnp.float32
BF16 = jnp.bfloat16

RMS_EPS = 1e-6
LN_EPS = 1e-5
CHUNK = 128
LANES = 128
HEAD_DIM = 64
N_KV_HEADS = 2
HEADS_PER_KV = 8
PAIRS_PER_KV = HEADS_PER_KV // 2
ROW_TILE = 512
WIDE_TILE = 1024
SUB_TILE = 256
BF16_SUBLANES = 16
CONVERT_STEPS = 16
VMEM_LIMIT = 56 * 1024 * 1024
NEG_INF = float("-inf")


def _rms(x, g):
    ms = jnp.mean(x * x, axis=-1, keepdims=True)
    return x * lax.rsqrt(ms + RMS_EPS) * g


def _const_spec(shape):
    zeros = (0,) * len(shape)
    return pl.BlockSpec(shape, lambda i: zeros, pipeline_mode=pl.Buffered(1))


HBM_SPEC = pl.BlockSpec(memory_space=pl.ANY)


def _convert_pieces(sources, grid_steps):
    if not sources:
        return [], [], [], []
    assert grid_steps > CONVERT_STEPS, "the last chunk's write-back is waited on the step after it"
    n = len(sources)
    scratch = []
    for stack, _ in sources:
        rows, cols = stack.shape[1] // CONVERT_STEPS, stack.shape[2]
        assert rows * CONVERT_STEPS == stack.shape[1] and rows % BF16_SUBLANES == 0
        scratch += [pltpu.VMEM((rows, cols), F32), pltpu.VMEM((rows, cols), BF16)]
    scratch += [pltpu.SemaphoreType.DMA((n,)), pltpu.SemaphoreType.DMA((n,))]
    out_shape = [jax.ShapeDtypeStruct(stack.shape[1:], BF16) for stack, _ in sources]
    return [HBM_SPEC] * n, [HBM_SPEC] * n, out_shape, scratch


def _convert_ahead(i, layers, srcs, dsts, scratch):
    n = len(layers)
    if n == 0:
        return
    sem_read, sem_write = scratch[2 * n:]

    def read(k, c):
        stage = scratch[2 * k]
        rows = stage.shape[0]
        return pltpu.make_async_copy(srcs[k].at[layers[k], pl.ds(c * rows, rows), :], stage, sem_read.at[k])

    def write(k, c):
        stage = scratch[2 * k + 1]
        rows = stage.shape[0]
        return pltpu.make_async_copy(stage, dsts[k].at[pl.ds(c * rows, rows), :], sem_write.at[k])

    @pl.when(i == 0)
    def _():
        for k in range(n):
            read(k, 0).start()

    @pl.when(jnp.logical_and(i > 0, i <= CONVERT_STEPS))
    def _():
        for k in range(n):
            write(k, i - 1).wait()

    @pl.when(i < CONVERT_STEPS)
    def _():
        for k in range(n):
            read(k, i).wait()
            scratch[2 * k + 1][...] = scratch[2 * k][...].astype(BF16)
            write(k, i).start()

    @pl.when(i + 1 < CONVERT_STEPS)
    def _():
        for k in range(n):
            read(k, i + 1).start()


def _row_spec(tm, width):
    return pl.BlockSpec((tm, width), lambda i: (i, 0))


def _params():
    return pltpu.CompilerParams(dimension_semantics=("arbitrary",), vmem_limit_bytes=VMEM_LIMIT)


def _mixer_kernel(*refs, n_prompt_tiles, split_input, convert):
    n_x = 2 if split_input else 1
    n_c = len(convert)
    gpre_ref, win_ref, lng_ref, lnb_ref, wmix_ref, bmix_ref, wout_ref, gpost_ref = refs[n_x:n_x + 8]
    conv_srcs = refs[n_x + 8:n_x + 8 + n_c]
    xo_ref, vsp_ref, vss_ref = refs[n_x + 8 + n_c:n_x + 11 + n_c]
    conv_dsts = refs[n_x + 11 + n_c:n_x + 11 + 2 * n_c]
    y_s = refs[n_x + 11 + 2 * n_c]
    i = pl.program_id(0)
    tm, width = y_s.shape
    sub = SUB_TILE
    _convert_ahead(i, convert, conv_srcs, conv_dsts, refs[n_x + 12 + 2 * n_c:])

    def project_in(rows):
        if split_input:
            x = jnp.where(i < n_prompt_tiles, refs[0][rows, :], refs[1][rows, :])
        else:
            x = refs[0][rows, :]
        h = _rms(x, gpre_ref[...]).astype(BF16)
        return x, jnp.dot(h, win_ref[...], preferred_element_type=F32)

    def gate_and_norm(rows, z):
        z = 0.5 * z * (1.0 + lax.erf(z * (0.5 ** 0.5)))
        u = z[:, :width]
        v = z[:, width:]
        mu = jnp.mean(v, axis=-1, keepdims=True)
        vc = v - mu
        var = jnp.mean(vc * vc, axis=-1, keepdims=True)
        v = vc * lax.rsqrt(var + LN_EPS) * lng_ref[...] + lnb_ref[...]
        if rows.stop == tm:
            vsp_ref[...] = v[sub - CHUNK:, :]
        vss_ref[rows, :] = v
        return u, v.astype(BF16)

    def mix_and_project_out(rows, x, u, vb):
        bias = bmix_ref[...]
        for c in range(sub // CHUNK):
            r = slice(c * CHUNK, (c + 1) * CHUNK)
            yr = slice(rows.start + c * CHUNK, rows.start + (c + 1) * CHUNK)
            for g in range(width // LANES):
                cols = slice(g * LANES, (g + 1) * LANES)
                mixed = jnp.dot(wmix_ref[g], vb[r, cols], preferred_element_type=F32)
                y_s[yr, cols] = (u[r, cols] * (mixed + bias[:, cols])).astype(BF16)
        out = jnp.dot(y_s[rows, :], wout_ref[...], preferred_element_type=F32)
        xo_ref[rows, :] = x + _rms(out, gpost_ref[...])

    parts = [slice(s, s + sub) for s in range(0, tm, sub)]
    ahead = project_in(parts[0])
    for n, rows in enumerate(parts):
        x, z = ahead
        if n + 1 < len(parts):
            ahead = project_in(parts[n + 1])
        u, vb = gate_and_norm(rows, z)
        mix_and_project_out(rows, x, u, vb)


def _mixer(xs, layer, gpre, win, lng, lnb, wmix, bmix, wout, gpost, *, n_prompt_rows, n_sample_rows,
           prompt_batch, convert):
    d = xs[0].shape[1]
    width = wout.shape[0]
    tm = WIDE_TILE
    npt = n_prompt_rows // tm
    tpb = npt // prompt_batch
    groups = wmix.shape[1]
    split_input = len(xs) == 2
    if split_input:
        x_specs = [pl.BlockSpec((tm, d), lambda i: (jnp.minimum(i, npt - 1), 0)),
                   pl.BlockSpec((tm, d), lambda i: (jnp.maximum(i - npt, 0), 0))]
    else:
        x_specs = [_row_spec(tm, d)]
    grid_steps = (n_prompt_rows + n_sample_rows) // tm
    conv_in, conv_out, conv_shape, conv_scratch = _convert_pieces(convert, grid_steps)
    kern = functools.partial(_mixer_kernel, n_prompt_tiles=npt, split_input=split_input,
                             convert=tuple(l for _, l in convert))
    out = pl.pallas_call(
        kern,
        grid=(grid_steps,),
        in_specs=x_specs + [
            _const_spec((1, d)),
            _const_spec(win.shape),
            _const_spec((1, width)),
            _const_spec((1, width)),
            pl.BlockSpec((None, groups, CHUNK, CHUNK), lambda i: (2 * layer + i // npt, 0, 0, 0)),
            pl.BlockSpec((None, CHUNK, width), lambda i: (2 * layer + i // npt, 0, 0)),
            _const_spec(wout.shape),
            _const_spec((1, d)),
        ] + conv_in,
        out_specs=[
            _row_spec(tm, d),
            pl.BlockSpec((None, CHUNK, width), lambda i: (jnp.minimum(i // tpb, prompt_batch), 0, 0)),
            pl.BlockSpec((tm, width), lambda i: (jnp.maximum(i - npt, 0), 0)),
        ] + conv_out,
        out_shape=[
            jax.ShapeDtypeStruct((n_prompt_rows + n_sample_rows, d), F32),
            jax.ShapeDtypeStruct((prompt_batch + 1, CHUNK, width), F32),
            jax.ShapeDtypeStruct((n_sample_rows, width), F32),
        ] + conv_shape,
        scratch_shapes=[pltpu.VMEM((tm, width), BF16)] + conv_scratch,
        compiler_params=_params(),
        name="gmlp_mixer",
    )(*xs, gpre, win, lng, lnb, wmix, bmix, wout, gpost, *[stack for stack, _ in convert])
    return out[:3], out[3:]


def _ffn_kernel(*refs, with_kv, sample_first, convert):
    x_ref, gpre_ref, wg_ref, wu_ref, wd_ref, gpost_ref = refs[:6]
    n_in = 9 if with_kv else 6
    n_out = 3 if with_kv else (2 if sample_first else 1)
    n_c = len(convert)
    conv_srcs = refs[n_in:n_in + n_c]
    outs = refs[n_in + n_c:n_in + n_c + n_out]
    conv_dsts = refs[n_in + n_c + n_out:n_in + 2 * n_c + n_out]
    ys_ref = None
    if with_kv:
        kvg_ref, wkv_ref, bkv_ref = refs[6:9]
        xo_ref, k_ref, v_ref = outs
    elif sample_first:
        xo_ref, ys_ref = outs
    else:
        (xo_ref,) = outs
    i = pl.program_id(0)
    tm = x_ref.shape[0]
    _convert_ahead(i, convert, conv_srcs, conv_dsts, refs[n_in + 2 * n_c + n_out:])

    def project_up(rows):
        x = x_ref[rows, :]
        h = _rms(x, gpre_ref[...]).astype(BF16)
        g = jnp.dot(h, wg_ref[...], preferred_element_type=F32)
        u = jnp.dot(h, wu_ref[...], preferred_element_type=F32)
        return x, g, u

    def gate_and_project_down(rows, x, g, u):
        a = (g * (1.0 / (1.0 + jnp.exp(-g))) * u).astype(BF16)
        out = jnp.dot(a, wd_ref[...], preferred_element_type=F32)
        xo_ref[rows, :] = x + _rms(out, gpost_ref[...])

    def project_kv(rows):
        hk = _rms(xo_ref[rows, :], kvg_ref[...]).astype(BF16)
        kv = jnp.dot(hk, wkv_ref[...], preferred_element_type=F32) + bkv_ref[...]
        half = k_ref.shape[1]
        k_ref[rows, :] = kv[:, :half]
        v_ref[rows, :] = kv[:, half:]

    parts = [slice(s, s + SUB_TILE) for s in range(0, tm, SUB_TILE)]
    ahead = project_up(parts[0])
    for n, rows in enumerate(parts):
        x, g, u = ahead
        if n + 1 < len(parts):
            ahead = project_up(parts[n + 1])
        gate_and_project_down(rows, x, g, u)
        if with_kv and n > 0:
            project_kv(parts[n - 1])
    if with_kv:
        project_kv(parts[-1])

    if sample_first:
        @pl.when(i == 0)
        def _():
            ys_ref[...] = xo_ref[...]


def _ffn(x, gpre, wg, wu, wd, gpost, kv_weights=None, n_prompt_rows=None, convert=()):
    rows, d = x.shape
    tm = WIDE_TILE
    with_kv = kv_weights is not None
    sample_first = n_prompt_rows is not None
    weight_specs = [_const_spec((1, d)), _const_spec(wg.shape), _const_spec(wu.shape), _const_spec(wd.shape),
                    _const_spec((1, d))]
    args = [x, gpre, wg, wu, wd, gpost]
    if sample_first:
        npt = n_prompt_rows // tm
        assert rows - n_prompt_rows == tm, "the sample rows must fill exactly one tile"
        in_specs = [pl.BlockSpec((tm, d), lambda i: (jnp.where(i == 0, npt, i - 1), 0))] + weight_specs
        out_specs = [pl.BlockSpec((tm, d), lambda i: (jnp.maximum(i - 1, 0), 0)),
                     pl.BlockSpec((tm, d), lambda i: (0, 0))]
        out_shape = [jax.ShapeDtypeStruct((n_prompt_rows, d), F32), jax.ShapeDtypeStruct((tm, d), F32)]
    else:
        in_specs = [_row_spec(tm, d)] + weight_specs
        out_specs = [_row_spec(tm, d)]
        out_shape = [jax.ShapeDtypeStruct((rows, d), F32)]
    if with_kv:
        kvg, wkv, bkv = kv_weights
        half = wkv.shape[1] // 2
        in_specs += [_const_spec((1, d)), _const_spec((d, 2 * half)), _const_spec((1, 2 * half))]
        out_specs += [_row_spec(tm, half)] * 2
        out_shape += [jax.ShapeDtypeStruct((rows, half), F32)] * 2
        args += [kvg, wkv, bkv]
    conv_in, conv_out, conv_shape, conv_scratch = _convert_pieces(convert, rows // tm)
    n_out = len(out_shape)
    out = pl.pallas_call(
        functools.partial(_ffn_kernel, with_kv=with_kv, sample_first=sample_first,
                          convert=tuple(l for _, l in convert)),
        grid=(rows // tm,),
        in_specs=in_specs + conv_in,
        out_specs=out_specs + conv_out,
        out_shape=out_shape + conv_shape,
        scratch_shapes=conv_scratch,
        compiler_params=_params(),
        name="swiglu_ffn_kv" if with_kv else "swiglu_ffn",
    )(*args, *[stack for stack, _ in convert])
    return out[:n_out], out[n_out:]


def _split_heads(k):
    kr = pltpu.roll(k, HEAD_DIM, k.ndim - 1)
    low = lax.broadcasted_iota(jnp.int32, k.shape, k.ndim - 1) < HEAD_DIM
    zero = jnp.zeros_like(k)
    head0 = (jnp.where(low, k, zero).astype(BF16), jnp.where(low, zero, kr).astype(BF16))
    head1 = (jnp.where(low, kr, zero).astype(BF16), jnp.where(low, zero, k).astype(BF16))
    return head0, head1


def _split_heads_t(v):
    vt = v.T
    vt_swapped = jnp.concatenate([vt[HEAD_DIM:], vt[:HEAD_DIM]], axis=0)
    top = lax.broadcasted_iota(jnp.int32, vt.shape, 0) < HEAD_DIM
    zero = jnp.zeros_like(vt)
    head0 = (jnp.where(top, vt, zero).astype(BF16), jnp.where(top, zero, vt_swapped).astype(BF16))
    head1 = (jnp.where(top, vt_swapped, zero).astype(BF16), jnp.where(top, zero, vt).astype(BF16))
    return head0, head1


def _sink_column(sinks_ref, kh, parity, rows_per_pair, lead):
    pieces = []
    for j in range(PAIRS_PER_KV):
        s = sinks_ref[kh * HEADS_PER_KV + 2 * j + parity]
        pieces.append(jnp.full(lead + (rows_per_pair, 1), s, F32))
    return jnp.concatenate(pieces, axis=len(lead))


def _sink_row(sinks_ref, kh, parity):
    pieces = [jnp.full((1, CHUNK), sinks_ref[kh * HEADS_PER_KV + 2 * j + parity], F32)
              for j in range(PAIRS_PER_KV)]
    return jnp.concatenate(pieces, axis=1)


def _prompt_attention(i, q_s, att_s, sinks_ref, k_own, k_prev, v_own, v_prev, tiles_per_batch,
                      project_q, project_o):
    tm = q_s.shape[0]
    half_blocks = tm // CHUNK // 2
    project_q(slice(0, tm // 2))
    project_q(slice(tm // 2, tm))
    kv_lanes = PAIRS_PER_KV * LANES
    n_q = PAIRS_PER_KV * CHUNK
    k_own = _split_heads(k_own)
    k_prev = _split_heads(k_prev)
    v_own = _split_heads_t(v_own)
    v_prev = _split_heads_t(v_prev)
    slot = lax.broadcasted_iota(jnp.int32, (2 * CHUNK, n_q), 0) & (CHUNK - 1)
    t = lax.broadcasted_iota(jnp.int32, (2 * CHUNK, n_q), 1) & (CHUNK - 1)
    own = slot <= t
    no_prev = jnp.where((i % tiles_per_batch) == 0, NEG_INF, 0.0)
    top = lax.broadcasted_iota(jnp.int32, (LANES, n_q), 0) < HEAD_DIM

    def scores(blk, kh):
        rows = slice(blk * CHUNK, (blk + 1) * CHUNK)
        prow = slice((blk - 1) * CHUNK, blk * CHUNK)
        ka_o, kb_o = k_own[kh][0][rows], k_own[kh][1][rows]
        ka_p, kb_p = k_prev[kh] if blk == 0 else (k_own[kh][0][prow], k_own[kh][1][prow])
        keys = jnp.concatenate([ka_o, kb_o, ka_p, kb_p], axis=0)
        qs = jnp.concatenate(
            [q_s[rows, kh * kv_lanes + j * LANES: kh * kv_lanes + (j + 1) * LANES]
             for j in range(PAIRS_PER_KV)], axis=0).astype(BF16)
        s4 = lax.dot_general(keys, qs, (((1,), (1,)), ((), ())), preferred_element_type=F32)
        s_prev = s4[2 * CHUNK:]
        if blk == 0:
            s_prev = s_prev + no_prev
        return jnp.where(own, s4[:2 * CHUNK], s_prev)

    def finish(blk, kh, s):
        rows = slice(blk * CHUNK, (blk + 1) * CHUNK)
        prow = slice((blk - 1) * CHUNK, blk * CHUNK)
        va_o, vb_o = v_own[kh][0][:, rows], v_own[kh][1][:, rows]
        va_p, vb_p = v_prev[kh] if blk == 0 else (v_own[kh][0][:, prow], v_own[kh][1][:, prow])
        vals_t = jnp.concatenate([va_o, vb_o, va_p, vb_p], axis=1)
        sa, sb = s[:CHUNK], s[CHUNK:]
        sink_a = _sink_row(sinks_ref, kh, 0)
        sink_b = _sink_row(sinks_ref, kh, 1)
        ma = jnp.maximum(jnp.max(sa, axis=0, keepdims=True), sink_a)
        mb = jnp.maximum(jnp.max(sb, axis=0, keepdims=True), sink_b)
        pa = jnp.exp(sa - ma)
        pb = jnp.exp(sb - mb)
        inv_a = 1.0 / (jnp.sum(pa, axis=0, keepdims=True) + jnp.exp(sink_a - ma))
        inv_b = 1.0 / (jnp.sum(pb, axis=0, keepdims=True) + jnp.exp(sink_b - mb))
        p = jnp.concatenate([pa, pb], axis=0)
        p4 = jnp.concatenate([jnp.where(own, p, 0.0), jnp.where(own, 0.0, p)], axis=0).astype(BF16)
        o_t = jnp.dot(vals_t, p4, preferred_element_type=F32)
        o_t = o_t * jnp.where(top, inv_a, inv_b)
        for j in range(PAIRS_PER_KV):
            att_s[rows, kh * kv_lanes + j * LANES: kh * kv_lanes + (j + 1) * LANES] = (
                o_t[:, j * CHUNK:(j + 1) * CHUNK].T.astype(BF16))

    items = [(blk, kh) for blk in range(tm // CHUNK) for kh in range(N_KV_HEADS)]
    s_next = scores(*items[0])
    for n, item in enumerate(items):
        s_cur = s_next
        if n + 1 < len(items):
            s_next = scores(*items[n + 1])
        finish(*item, s_cur)
        blk, kh = item
        if kh == N_KV_HEADS - 1 and (blk + 1) % half_blocks == 0:
            first = (blk + 1 - half_blocks) * CHUNK
            project_o(slice(first, first + tm // 2))


def _sample_attention(q_s, att_s, sinks_ref, k_new, v_new, k_old, v_old, dec_seq):
    tm, d = q_s.shape
    kv_lanes = PAIRS_PER_KV * LANES
    nb = tm // dec_seq
    m_rows = PAIRS_PER_KV * dec_seq
    window = k_old.shape[1]
    k_new = _split_heads(k_new.reshape(nb, dec_seq, LANES))
    v_new = _split_heads(v_new.reshape(nb, dec_seq, LANES))
    k_old = _split_heads(k_old.astype(F32))
    v_old = _split_heads(v_old.astype(F32))
    t_c = lax.broadcasted_iota(jnp.int32, (nb, m_rows, 2 * window), 1) & (dec_seq - 1)
    c_c = lax.broadcasted_iota(jnp.int32, (nb, m_rows, 2 * window), 2) & (window - 1)
    valid_c = c_c > t_c
    t_n = lax.broadcasted_iota(jnp.int32, (nb, m_rows, 2 * dec_seq), 1) & (dec_seq - 1)
    col_n = lax.broadcasted_iota(jnp.int32, (nb, m_rows, 2 * dec_seq), 2)
    valid_n = (col_n & (dec_seq - 1)) <= t_n
    first_n = col_n < dec_seq
    lane_low = lax.broadcasted_iota(jnp.int32, (nb, m_rows, LANES), 2) < HEAD_DIM
    q3 = q_s[...].reshape(nb, dec_seq, d)
    bmm_nt = (((2,), (2,)), ((0,), (0,)))
    bmm_nn = (((2,), (1,)), ((0,), (0,)))
    for kh in range(N_KV_HEADS):
        rhs_c = jnp.concatenate([k_old[kh][0], k_old[kh][1]], axis=1)
        vv_c = jnp.concatenate([v_old[kh][0], v_old[kh][1]], axis=1)
        rhs_n = jnp.concatenate([k_new[kh][0].astype(F32), k_new[kh][1].astype(F32)], axis=1).astype(BF16)
        vv_n = jnp.concatenate([v_new[kh][0].astype(F32), v_new[kh][1].astype(F32)], axis=1).astype(BF16)
        qs = jnp.concatenate(
            [q3[:, :, kh * kv_lanes + j * LANES: kh * kv_lanes + (j + 1) * LANES]
             for j in range(PAIRS_PER_KV)], axis=1).astype(BF16)
        s_c = lax.dot_general(qs, rhs_c, bmm_nt, preferred_element_type=F32)
        s_n = lax.dot_general(qs, rhs_n, bmm_nt, preferred_element_type=F32)
        s_c = jnp.where(valid_c, s_c, NEG_INF)
        s_n = jnp.where(valid_n, s_n, NEG_INF)
        sa, sb = s_c[:, :, :window], s_c[:, :, window:]
        sink_a = _sink_column(sinks_ref, kh, 0, dec_seq, (1,))
        sink_b = _sink_column(sinks_ref, kh, 1, dec_seq, (1,))
        na = jnp.max(jnp.where(first_n, s_n, NEG_INF), axis=2, keepdims=True)
        nb_ = jnp.max(jnp.where(first_n, NEG_INF, s_n), axis=2, keepdims=True)
        ma = jnp.maximum(jnp.maximum(jnp.max(sa, axis=2, keepdims=True), na), sink_a)
        mb = jnp.maximum(jnp.maximum(jnp.max(sb, axis=2, keepdims=True), nb_), sink_b)
        pa = jnp.exp(sa - ma)
        pb = jnp.exp(sb - mb)
        pn = jnp.exp(s_n - jnp.where(first_n, ma, mb))
        den_a = (jnp.sum(pa, axis=2, keepdims=True) + jnp.exp(sink_a - ma)
                 + jnp.sum(jnp.where(first_n, pn, 0.0), axis=2, keepdims=True))
        den_b = (jnp.sum(pb, axis=2, keepdims=True) + jnp.exp(sink_b - mb)
                 + jnp.sum(jnp.where(first_n, 0.0, pn), axis=2, keepdims=True))
        p_c = jnp.concatenate([pa, pb], axis=2).astype(BF16)
        o = lax.dot_general(p_c, vv_c, bmm_nn, preferred_element_type=F32)
        o = o + lax.dot_general(pn.astype(BF16), vv_n, bmm_nn, preferred_element_type=F32)
        o = o * jnp.where(lane_low, 1.0 / den_a, 1.0 / den_b)
        for j in range(PAIRS_PER_KV):
            att_s[:, kh * kv_lanes + j * LANES: kh * kv_lanes + (j + 1) * LANES] = (
                o[:, j * dec_seq:(j + 1) * dec_seq, :].reshape(tm, LANES).astype(BF16))


def _attn_kernel(*refs, n_prompt_tiles, tiles_per_batch, dec_seq, convert):
    (sinks_ref, x_ref, gpre_ref, wq_ref, bq_ref, k_ref, v_ref, kprev_ref, vprev_ref,
     ck_ref, cv_ref, wo_ref, bo_ref, gpost_ref) = refs[:14]
    n_c = len(convert)
    conv_srcs = refs[14:14 + n_c]
    xo_ref = refs[14 + n_c]
    conv_dsts = refs[15 + n_c:15 + 2 * n_c]
    q_s, att_s = refs[15 + 2 * n_c:17 + 2 * n_c]
    i = pl.program_id(0)
    tm = q_s.shape[0]
    _convert_ahead(i, convert, conv_srcs, conv_dsts, refs[17 + 2 * n_c:])

    def project_q(rows):
        h = _rms(x_ref[rows, :], gpre_ref[...]).astype(BF16)
        q = jnp.dot(h, wq_ref[...], preferred_element_type=F32) + bq_ref[...]
        q_s[rows, :] = q * (HEAD_DIM ** -0.5)

    def project_o(rows):
        out = jnp.dot(att_s[rows, :], wo_ref[...], preferred_element_type=F32) + bo_ref[...]
        xo_ref[rows, :] = x_ref[rows, :] + _rms(out, gpost_ref[...])

    @pl.when(i < n_prompt_tiles)
    def _():
        _prompt_attention(i, q_s, att_s, sinks_ref, k_ref[...], kprev_ref[...], v_ref[...], vprev_ref[...],
                          tiles_per_batch, project_q, project_o)

    @pl.when(i >= n_prompt_tiles)
    def _():
        project_q(slice(0, tm))
        _sample_attention(q_s, att_s, sinks_ref, k_ref[...], v_ref[...], ck_ref[...], cv_ref[...], dec_seq)
        project_o(slice(0, tm))


def _attn(x, k, v, ck, cv, sinks, gpre, wq, bq, wo, bo, gpost, *, n_prompt_rows, prompt_batch, dec_seq,
          convert):
    rows, d = x.shape
    tm = ROW_TILE
    npt = n_prompt_rows // tm
    tpb = npt // prompt_batch
    blocks_per_tile = tm // CHUNK
    nb = tm // dec_seq
    window = ck.shape[1]
    conv_in, conv_out, conv_shape, conv_scratch = _convert_pieces(convert, rows // tm)
    kern = functools.partial(_attn_kernel, n_prompt_tiles=npt, tiles_per_batch=tpb, dec_seq=dec_seq,
                             convert=tuple(l for _, l in convert))
    prev_spec = pl.BlockSpec((CHUNK, LANES), lambda i: (jnp.maximum(i * blocks_per_tile - 1, 0), 0))
    cache_spec = pl.BlockSpec((nb, window, LANES), lambda i: (jnp.maximum(i - npt, 0), 0, 0))
    out = pl.pallas_call(
        kern,
        grid=(rows // tm,),
        in_specs=[
            pl.BlockSpec(memory_space=pltpu.SMEM),
            _row_spec(tm, d),
            _const_spec((1, d)),
            _const_spec(wq.shape),
            _const_spec((1, d)),
            _row_spec(tm, LANES), _row_spec(tm, LANES), prev_spec, prev_spec,
            cache_spec, cache_spec,
            _const_spec(wo.shape),
            _const_spec((1, d)),
            _const_spec((1, d)),
        ] + conv_in,
        out_specs=[_row_spec(tm, d)] + conv_out,
        out_shape=[jax.ShapeDtypeStruct((rows, d), F32)] + conv_shape,
        scratch_shapes=[pltpu.VMEM((tm, d), F32), pltpu.VMEM((tm, d), BF16)] + conv_scratch,
        compiler_params=_params(),
        name="swa_attention",
    )(sinks, x, gpre, wq, bq, k, v, k, v, ck, cv, wo, bo, gpost, *[stack for stack, _ in convert])
    return out[0], out[1:]


def _mixing_weights(w_s, b_s, n):
    layers, groups = w_s.shape[:2]
    r = lax.broadcasted_iota(jnp.int32, (CHUNK, CHUNK), 0)
    c = lax.broadcasted_iota(jnp.int32, (CHUNK, CHUNK), 1)
    w_prompt = jnp.where(r >= c, w_s, 0.0)
    reps = CHUNK // n
    pick = (lax.broadcasted_iota(jnp.int32, (CHUNK, n), 0) % n
            == lax.broadcasted_iota(jnp.int32, (CHUNK, n), 1)).astype(F32)
    w_tiled = jnp.einsum('ra,lgab,cb->lgrc', pick, w_s[:, :, :n, :n], pick, precision=lax.Precision.HIGHEST)
    w_sample = jnp.where(jnp.logical_and(r >= c, r // n == c // n), w_tiled, 0.0)
    wmix = jnp.stack([w_prompt, w_sample], axis=1).astype(BF16)
    b_prompt = jnp.repeat(jnp.swapaxes(b_s, 1, 2), LANES, axis=2)
    b_sample = jnp.tile(b_prompt[:, :n], (1, reps, 1))
    bmix = jnp.stack([b_prompt, b_sample], axis=1)
    return wmix.reshape(layers * 2, groups, CHUNK, CHUNK), bmix.reshape(layers * 2, CHUNK, -1)


def kernel(x_prompt, x_sample, cache_k, cache_v, sg_norm_pre, sg_w_in, sg_ln_g, sg_ln_b, sg_w_s, sg_b_s, sg_w_out, sg_norm_post, kv_norm, w_kv, b_kv, sw_norm_pre, sw_w_q, sw_b_q, sw_sinks, sw_w_o, sw_b_o, sw_norm_post, f_norm_pre, f_w_gate, f_w_up, f_w_down, f_norm_post):
    bp, sp, d = x_prompt.shape
    bs, ss, _ = x_sample.shape
    n_a = sg_w_in.shape[0]
    n_b = sw_w_q.shape[0]
    depth = n_a + n_b
    n_p, n_s = bp * sp, bs * ss

    wmix, bmix = _mixing_weights(sg_w_s, sg_b_s, ss)
    row = lambda a: a.reshape(1, -1)
    kvh = cache_k.shape[2]
    ck = cache_k.reshape(bs, cache_k.shape[1], -1).astype(BF16)
    cv = cache_v.reshape(bs, cache_v.shape[1], -1).astype(BF16)

    def mixer_sources(l):
        return [(sg_w_in, l), (sg_w_out, l)]

    def attn_sources(l):
        return [(sw_w_q, l), (sw_w_o, l)]

    def ffn_sources(layer):
        return [(f_w_gate, layer), (f_w_up, layer), (f_w_down, layer)]

    def sources_after(layer, in_ffn):
        if not in_ffn:
            return ffn_sources(layer)
        if layer + 1 == depth:
            return []
        return mixer_sources(layer + 1) if layer + 1 < n_a else attn_sources(layer + 1 - n_a)

    xs = [x_prompt.reshape(n_p, d), x_sample.reshape(n_s, d)]
    v_prompt_rows, v_sample_rows = [], []
    k_all = v_all = y_prompt = y_sample = None
    weights = [stack[l].astype(BF16) for stack, l in (mixer_sources(0) if n_a else attn_sources(0))]
    for layer in range(depth):
        if layer < n_a:
            l = layer
            (x, vsp, vss), weights = _mixer(
                xs, l, row(sg_norm_pre[l]), weights[0], row(sg_ln_g[l]), row(sg_ln_b[l]), wmix, bmix, weights[1],
                row(sg_norm_post[l]), n_prompt_rows=n_p, n_sample_rows=n_s, prompt_batch=bp,
                convert=sources_after(layer, False))
            v_prompt_rows.append(vsp[:bp])
            v_sample_rows.append(vss.reshape(bs, ss, -1))
        else:
            l = layer - n_a
            x, weights = _attn(x, k_all, v_all, ck, cv, sw_sinks[l], row(sw_norm_pre[l]), weights[0],
                               row(sw_b_q[l]), weights[1], row(sw_b_o[l]), row(sw_norm_post[l]),
                               n_prompt_rows=n_p, prompt_batch=bp, dec_seq=ss, convert=sources_after(layer, False))
        ffn_args = (row(f_norm_pre[layer]), *weights, row(f_norm_post[layer]))
        nxt = sources_after(layer, True)
        if layer == n_a - 1:
            (x, k_all, v_all), weights = _ffn(
                x, *ffn_args, kv_weights=(row(kv_norm), w_kv.astype(BF16), row(b_kv)), convert=nxt)
        elif layer == depth - 1:
            (y_prompt, y_sample), weights = _ffn(x, *ffn_args, n_prompt_rows=n_p, convert=nxt)
        else:
            (x,), weights = _ffn(x, *ffn_args, convert=nxt)
        xs = [x]

    def last_window(a):
        return a[:n_p].reshape(bp, sp, -1)[:, sp - CHUNK:].reshape(bp, CHUNK, kvh, -1)

    return (y_prompt.reshape(bp, sp, d), y_sample.reshape(bs, ss, d),
            jnp.stack(v_prompt_rows, axis=0), jnp.stack(v_sample_rows, axis=0),
            last_window(k_all), last_window(v_all),
            k_all[n_p:].reshape(bs, ss, kvh, -1), v_all[n_p:].reshape(bs, ss, kvh, -1))
```

```python
import functools

import jax
import jax.numpy as jnp
from jax import lax
from jax.experimental import pallas as pl
from jax.experimental.pallas import tpu as pltpu

F32 = jnp.float32
BF16 = jnp.bfloat16

RMS_EPS = 1e-6
LN_EPS = 1e-5
CHUNK = 128
LANES = 128
HEAD_DIM = 64
N_KV_HEADS = 2
HEADS_PER_KV = 8
PAIRS_PER_KV = HEADS_PER_KV // 2
ROW_TILE = 512
WIDE_TILE = 1024
FFN_SUB_TILE = 256
MIXER_SUB_TILE = 512
BF16_SUBLANES = 16
CONVERT_STEPS = 16
VMEM_LIMIT = 56 * 1024 * 1024
NEG_INF = float("-inf")


def _rms(x, g):
    ms = jnp.mean(x * x, axis=-1, keepdims=True)
    return x * lax.rsqrt(ms + RMS_EPS) * g


def _const_spec(shape):
    zeros = (0,) * len(shape)
    return pl.BlockSpec(shape, lambda i: zeros, pipeline_mode=pl.Buffered(1))


HBM_SPEC = pl.BlockSpec(memory_space=pl.ANY)


def _convert_pieces(sources, grid_steps):
    if not sources:
        return [], [], [], []
    assert grid_steps > CONVERT_STEPS, "the last chunk's write-back is waited on the step after it"
    n = len(sources)
    scratch = []
    for stack, _ in sources:
        rows, cols = stack.shape[1] // CONVERT_STEPS, stack.shape[2]
        assert rows * CONVERT_STEPS == stack.shape[1] and rows % BF16_SUBLANES == 0
        scratch += [pltpu.VMEM((rows, cols), F32), pltpu.VMEM((rows, cols), BF16)]
    scratch += [pltpu.SemaphoreType.DMA((n,)), pltpu.SemaphoreType.DMA((n,))]
    out_shape = [jax.ShapeDtypeStruct(stack.shape[1:], BF16) for stack, _ in sources]
    return [HBM_SPEC] * n, [HBM_SPEC] * n, out_shape, scratch


def _convert_ahead(i, layers, srcs, dsts, scratch):
    n = len(layers)
    if n == 0:
        return
    sem_read, sem_write = scratch[2 * n:]

    def read(k, c):
        stage = scratch[2 * k]
        rows = stage.shape[0]
        return pltpu.make_async_copy(srcs[k].at[layers[k], pl.ds(c * rows, rows), :], stage, sem_read.at[k])

    def write(k, c):
        stage = scratch[2 * k + 1]
        rows = stage.shape[0]
        return pltpu.make_async_copy(stage, dsts[k].at[pl.ds(c * rows, rows), :], sem_write.at[k])

    @pl.when(i == 0)
    def _():
        for k in range(n):
            read(k, 0).start()

    @pl.when(jnp.logical_and(i > 0, i <= CONVERT_STEPS))
    def _():
        for k in range(n):
            write(k, i - 1).wait()

    @pl.when(i < CONVERT_STEPS)
    def _():
        for k in range(n):
            read(k, i).wait()
            scratch[2 * k + 1][...] = scratch[2 * k][...].astype(BF16)
            write(k, i).start()

    @pl.when(i + 1 < CONVERT_STEPS)
    def _():
        for k in range(n):
            read(k, i + 1).start()


def _row_spec(tm, width):
    return pl.BlockSpec((tm, width), lambda i: (i, 0))


def _params():
    return pltpu.CompilerParams(dimension_semantics=("arbitrary",), vmem_limit_bytes=VMEM_LIMIT)


def _mixer_kernel(*refs, n_prompt_tiles, split_input, convert):
    n_x = 2 if split_input else 1
    n_c = len(convert)
    gpre_ref, win_ref, lng_ref, lnb_ref, wmix_ref, bmix_ref, wout_ref, gpost_ref = refs[n_x:n_x + 8]
    conv_srcs = refs[n_x + 8:n_x + 8 + n_c]
    xo_ref, vsp_ref, vss_ref = refs[n_x + 8 + n_c:n_x + 11 + n_c]
    conv_dsts = refs[n_x + 11 + n_c:n_x + 11 + 2 * n_c]
    y_s = refs[n_x + 11 + 2 * n_c]
    i = pl.program_id(0)
    tm, width = y_s.shape
    sub = MIXER_SUB_TILE
    _convert_ahead(i, convert, conv_srcs, conv_dsts, refs[n_x + 12 + 2 * n_c:])

    def gelu(z):
        return 0.5 * z * (1.0 + lax.erf(z * (0.5 ** 0.5)))

    def normed_input(rows):
        if split_input:
            x = jnp.where(i < n_prompt_tiles, refs[0][rows, :], refs[1][rows, :])
        else:
            x = refs[0][rows, :]
        return x, _rms(x, gpre_ref[...]).astype(BF16)

    def project_v(h):
        return jnp.dot(h, win_ref[:, width:], preferred_element_type=F32)

    def project_u(h):
        return jnp.dot(h, win_ref[:, :width], preferred_element_type=F32)

    def norm_v(rows, zv):
        v = gelu(zv)
        mu = jnp.mean(v, axis=-1, keepdims=True)
        vc = v - mu
        var = jnp.mean(vc * vc, axis=-1, keepdims=True)
        v = vc * lax.rsqrt(var + LN_EPS) * lng_ref[...] + lnb_ref[...]
        if rows.stop == tm:
            vsp_ref[...] = v[sub - CHUNK:, :]
        vss_ref[rows, :] = v
        return v.astype(BF16)

    def mix_and_project_out(rows, x, u, vb):
        bias = bmix_ref[...]
        for c in range(sub // CHUNK):
            r = slice(c * CHUNK, (c + 1) * CHUNK)
            yr = slice(rows.start + c * CHUNK, rows.start + (c + 1) * CHUNK)
            for g in range(width // LANES):
                cols = slice(g * LANES, (g + 1) * LANES)
                mixed = jnp.dot(wmix_ref[g], vb[r, cols], preferred_element_type=F32)
                y_s[yr, cols] = (u[r, cols] * (mixed + bias[:, cols])).astype(BF16)
        out = jnp.dot(y_s[rows, :], wout_ref[...], preferred_element_type=F32)
        xo_ref[rows, :] = x + _rms(out, gpost_ref[...])

    parts = [slice(s, s + sub) for s in range(0, tm, sub)]
    x, h = normed_input(parts[0])
    ahead = (x, project_v(h), project_u(h))
    for n, rows in enumerate(parts):
        x, zv, zu = ahead
        if n + 1 < len(parts):
            x_next, h_next = normed_input(parts[n + 1])
            zv_next = project_v(h_next)
        vb = norm_v(rows, zv)
        if n + 1 < len(parts):
            ahead = (x_next, zv_next, project_u(h_next))
        u = gelu(zu)
        mix_and_project_out(rows, x, u, vb)


def _mixer(xs, layer, gpre, win, lng, lnb, wmix, bmix, wout, gpost, *, n_prompt_rows, n_sample_rows,
           prompt_batch, convert):
    d = xs[0].shape[1]
    width = wout.shape[0]
    tm = WIDE_TILE
    npt = n_prompt_rows // tm
    tpb = npt // prompt_batch
    groups = wmix.shape[1]
    split_input = len(xs) == 2
    if split_input:
        x_specs = [pl.BlockSpec((tm, d), lambda i: (jnp.minimum(i, npt - 1), 0)),
                   pl.BlockSpec((tm, d), lambda i: (jnp.maximum(i - npt, 0), 0),
                                pipeline_mode=pl.Buffered(1 if n_sample_rows == tm else 2))]
    else:
        x_specs = [_row_spec(tm, d)]
    grid_steps = (n_prompt_rows + n_sample_rows) // tm
    conv_in, conv_out, conv_shape, conv_scratch = _convert_pieces(convert, grid_steps)
    kern = functools.partial(_mixer_kernel, n_prompt_tiles=npt, split_input=split_input,
                             convert=tuple(l for _, l in convert))
    out = pl.pallas_call(
        kern,
        grid=(grid_steps,),
        in_specs=x_specs + [
            _const_spec((1, d)),
            _const_spec(win.shape),
            _const_spec((1, width)),
            _const_spec((1, width)),
            pl.BlockSpec((None, groups, CHUNK, CHUNK), lambda i: (2 * layer + i // npt, 0, 0, 0)),
            pl.BlockSpec((None, CHUNK, width), lambda i: (2 * layer + i // npt, 0, 0)),
            _const_spec(wout.shape),
            _const_spec((1, d)),
        ] + conv_in,
        out_specs=[
            _row_spec(tm, d),
            pl.BlockSpec((None, CHUNK, width), lambda i: (jnp.minimum(i // tpb, prompt_batch), 0, 0)),
            pl.BlockSpec((tm, width), lambda i: (jnp.maximum(i - npt, 0), 0)),
        ] + conv_out,
        out_shape=[
            jax.ShapeDtypeStruct((n_prompt_rows + n_sample_rows, d), F32),
            jax.ShapeDtypeStruct((prompt_batch + 1, CHUNK, width), F32),
            jax.ShapeDtypeStruct((n_sample_rows, width), F32),
        ] + conv_shape,
        scratch_shapes=[pltpu.VMEM((tm, width), BF16)] + conv_scratch,
        compiler_params=_params(),
        name="gmlp_mixer",
    )(*xs, gpre, win, lng, lnb, wmix, bmix, wout, gpost, *[stack for stack, _ in convert])
    return out[:3], out[3:]


def _ffn_kernel(*refs, with_kv, sample_first, convert):
    x_ref, gpre_ref, wg_ref, wu_ref, wd_ref, gpost_ref = refs[:6]
    n_in = 9 if with_kv else 6
    n_out = 3 if with_kv else (2 if sample_first else 1)
    n_c = len(convert)
    conv_srcs = refs[n_in:n_in + n_c]
    outs = refs[n_in + n_c:n_in + n_c + n_out]
    conv_dsts = refs[n_in + n_c + n_out:n_in + 2 * n_c + n_out]
    ys_ref = None
    if with_kv:
        kvg_ref, wkv_ref, bkv_ref = refs[6:9]
        xo_ref, k_ref, v_ref = outs
    elif sample_first:
        xo_ref, ys_ref = outs
    else:
        (xo_ref,) = outs
    i = pl.program_id(0)
    tm = x_ref.shape[0]
    _convert_ahead(i, convert, conv_srcs, conv_dsts, refs[n_in + 2 * n_c + n_out:])

    def project_up(rows):
        x = x_ref[rows, :]
        h = _rms(x, gpre_ref[...]).astype(BF16)
        g = jnp.dot(h, wg_ref[...], preferred_element_type=F32)
        u = jnp.dot(h, wu_ref[...], preferred_element_type=F32)
        return x, g, u

    def gate_and_project_down(rows, x, g, u):
        a = (g * (1.0 / (1.0 + jnp.exp(-g))) * u).astype(BF16)
        out = jnp.dot(a, wd_ref[...], preferred_element_type=F32)
        xo_ref[rows, :] = x + _rms(out, gpost_ref[...])

    def project_kv(rows):
        hk = _rms(xo_ref[rows, :], kvg_ref[...]).astype(BF16)
        kv = jnp.dot(hk, wkv_ref[...], preferred_element_type=F32) + bkv_ref[...]
        half = k_ref.shape[1]
        k_ref[rows, :] = kv[:, :half]
        v_ref[rows, :] = kv[:, half:]

    parts = [slice(s, s + FFN_SUB_TILE) for s in range(0, tm, FFN_SUB_TILE)]
    ahead = project_up(parts[0])
    for n, rows in enumerate(parts):
        x, g, u = ahead
        if n + 1 < len(parts):
            ahead = project_up(parts[n + 1])
        gate_and_project_down(rows, x, g, u)
        if with_kv and n > 0:
            project_kv(parts[n - 1])
    if with_kv:
        project_kv(parts[-1])

    if sample_first:
        @pl.when(i == 0)
        def _():
            ys_ref[...] = xo_ref[...]


def _ffn(x, gpre, wg, wu, wd, gpost, kv_weights=None, n_prompt_rows=None, convert=()):
    rows, d = x.shape
    tm = WIDE_TILE
    with_kv = kv_weights is not None
    sample_first = n_prompt_rows is not None
    weight_specs = [_const_spec((1, d)), _const_spec(wg.shape), _const_spec(wu.shape), _const_spec(wd.shape),
                    _const_spec((1, d))]
    args = [x, gpre, wg, wu, wd, gpost]
    if sample_first:
        npt = n_prompt_rows // tm
        assert rows - n_prompt_rows == tm, "the sample rows must fill exactly one tile"
        in_specs = [pl.BlockSpec((tm, d), lambda i: (jnp.where(i == 0, npt, i - 1), 0))] + weight_specs
        out_specs = [pl.BlockSpec((tm, d), lambda i: (jnp.maximum(i - 1, 0), 0)),
                     pl.BlockSpec((tm, d), lambda i: (0, 0))]
        out_shape = [jax.ShapeDtypeStruct((n_prompt_rows, d), F32), jax.ShapeDtypeStruct((tm, d), F32)]
    else:
        in_specs = [_row_spec(tm, d)] + weight_specs
        out_specs = [_row_spec(tm, d)]
        out_shape = [jax.ShapeDtypeStruct((rows, d), F32)]
    if with_kv:
        kvg, wkv, bkv = kv_weights
        half = wkv.shape[1] // 2
        in_specs += [_const_spec((1, d)), _const_spec((d, 2 * half)), _const_spec((1, 2 * half))]
        out_specs += [_row_spec(tm, half)] * 2
        out_shape += [jax.ShapeDtypeStruct((rows, half), F32)] * 2
        args += [kvg, wkv, bkv]
    conv_in, conv_out, conv_shape, conv_scratch = _convert_pieces(convert, rows // tm)
    n_out = len(out_shape)
    out = pl.pallas_call(
        functools.partial(_ffn_kernel, with_kv=with_kv, sample_first=sample_first,
                          convert=tuple(l for _, l in convert)),
        grid=(rows // tm,),
        in_specs=in_specs + conv_in,
        out_specs=out_specs + conv_out,
        out_shape=out_shape + conv_shape,
        scratch_shapes=conv_scratch,
        compiler_params=_params(),
        name="swiglu_ffn_kv" if with_kv else "swiglu_ffn",
    )(*args, *[stack for stack, _ in convert])
    return out[:n_out], out[n_out:]


def _split_heads(k):
    kr = pltpu.roll(k, HEAD_DIM, k.ndim - 1)
    low = lax.broadcasted_iota(jnp.int32, k.shape, k.ndim - 1) < HEAD_DIM
    zero = jnp.zeros_like(k)
    head0 = (jnp.where(low, k, zero).astype(BF16), jnp.where(low, zero, kr).astype(BF16))
    head1 = (jnp.where(low, kr, zero).astype(BF16), jnp.where(low, zero, k).astype(BF16))
    return head0, head1


def _split_heads_t(v):
    vt = v.T
    vt_swapped = jnp.concatenate([vt[HEAD_DIM:], vt[:HEAD_DIM]], axis=0)
    top = lax.broadcasted_iota(jnp.int32, vt.shape, 0) < HEAD_DIM
    zero = jnp.zeros_like(vt)
    head0 = (jnp.where(top, vt, zero).astype(BF16), jnp.where(top, zero, vt_swapped).astype(BF16))
    head1 = (jnp.where(top, vt_swapped, zero).astype(BF16), jnp.where(top, zero, vt).astype(BF16))
    return head0, head1


def _sink_column(sinks_ref, kh, parity, rows_per_pair, lead):
    pieces = []
    for j in range(PAIRS_PER_KV):
        s = sinks_ref[kh * HEADS_PER_KV + 2 * j + parity]
        pieces.append(jnp.full(lead + (rows_per_pair, 1), s, F32))
    return jnp.concatenate(pieces, axis=len(lead))


def _sink_row(sinks_ref, kh, parity):
    pieces = [jnp.full((1, CHUNK), sinks_ref[kh * HEADS_PER_KV + 2 * j + parity], F32)
              for j in range(PAIRS_PER_KV)]
    return jnp.concatenate(pieces, axis=1)


def _prompt_attention(i, q_s, att_s, sinks_ref, k_own, k_prev, v_own, v_prev, tiles_per_batch):
    tm = q_s.shape[0]
    kv_lanes = PAIRS_PER_KV * LANES
    n_q = PAIRS_PER_KV * CHUNK
    k_own = _split_heads(k_own)
    k_prev = _split_heads(k_prev)
    v_own = _split_heads_t(v_own)
    v_prev = _split_heads_t(v_prev)
    slot = lax.broadcasted_iota(jnp.int32, (2 * CHUNK, n_q), 0) & (CHUNK - 1)
    t = lax.broadcasted_iota(jnp.int32, (2 * CHUNK, n_q), 1) & (CHUNK - 1)
    own = slot <= t
    no_prev = jnp.where((i % tiles_per_batch) == 0, NEG_INF, 0.0)
    top = lax.broadcasted_iota(jnp.int32, (LANES, n_q), 0) < HEAD_DIM

    def scores(blk, kh):
        rows = slice(blk * CHUNK, (blk + 1) * CHUNK)
        prow = slice((blk - 1) * CHUNK, blk * CHUNK)
        ka_o, kb_o = k_own[kh][0][rows], k_own[kh][1][rows]
        ka_p, kb_p = k_prev[kh] if blk == 0 else (k_own[kh][0][prow], k_own[kh][1][prow])
        keys = jnp.concatenate([ka_o, kb_o, ka_p, kb_p], axis=0)
        qs = jnp.concatenate(
            [q_s[rows, kh * kv_lanes + j * LANES: kh * kv_lanes + (j + 1) * LANES]
             for j in range(PAIRS_PER_KV)], axis=0).astype(BF16)
        s4 = lax.dot_general(keys, qs, (((1,), (1,)), ((), ())), preferred_element_type=F32)
        s_prev = s4[2 * CHUNK:]
        if blk == 0:
            s_prev = s_prev + no_prev
        return jnp.where(own, s4[:2 * CHUNK], s_prev)

    def finish(blk, kh, s):
        rows = slice(blk * CHUNK, (blk + 1) * CHUNK)
        prow = slice((blk - 1) * CHUNK, blk * CHUNK)
        va_o, vb_o = v_own[kh][0][:, rows], v_own[kh][1][:, rows]
        va_p, vb_p = v_prev[kh] if blk == 0 else (v_own[kh][0][:, prow], v_own[kh][1][:, prow])
        vals_t = jnp.concatenate([va_o, vb_o, va_p, vb_p], axis=1)
        sa, sb = s[:CHUNK], s[CHUNK:]
        sink_a = _sink_row(sinks_ref, kh, 0)
        sink_b = _sink_row(sinks_ref, kh, 1)
        ma = jnp.maximum(jnp.max(sa, axis=0, keepdims=True), sink_a)
        mb = jnp.maximum(jnp.max(sb, axis=0, keepdims=True), sink_b)
        pa = jnp.exp(sa - ma)
        pb = jnp.exp(sb - mb)
        inv_a = 1.0 / (jnp.sum(pa, axis=0, keepdims=True) + jnp.exp(sink_a - ma))
        inv_b = 1.0 / (jnp.sum(pb, axis=0, keepdims=True) + jnp.exp(sink_b - mb))
        p = jnp.concatenate([pa, pb], axis=0)
        p4 = jnp.concatenate([jnp.where(own, p, 0.0), jnp.where(own, 0.0, p)], axis=0).astype(BF16)
        o_t = jnp.dot(vals_t, p4, preferred_element_type=F32)
        o_t = o_t * jnp.where(top, inv_a, inv_b)
        for j in range(PAIRS_PER_KV):
            att_s[rows, kh * kv_lanes + j * LANES: kh * kv_lanes + (j + 1) * LANES] = (
                o_t[:, j * CHUNK:(j + 1) * CHUNK].T.astype(BF16))

    items = [(blk, kh) for blk in range(tm // CHUNK) for kh in range(N_KV_HEADS)]
    s_next = scores(*items[0])
    for n, item in enumerate(items):
        s_cur = s_next
        if n + 1 < len(items):
            s_next = scores(*items[n + 1])
        finish(*item, s_cur)


def _sample_attention(q_s, att_s, sinks_ref, k_new, v_new, k_old, v_old, dec_seq):
    tm, d = q_s.shape
    kv_lanes = PAIRS_PER_KV * LANES
    nb = tm // dec_seq
    m_rows = PAIRS_PER_KV * dec_seq
    window = k_old.shape[1]
    k_new = _split_heads(k_new.reshape(nb, dec_seq, LANES))
    v_new = _split_heads(v_new.reshape(nb, dec_seq, LANES))
    k_old = _split_heads(k_old.astype(F32))
    v_old = _split_heads(v_old.astype(F32))
    t_c = lax.broadcasted_iota(jnp.int32, (nb, m_rows, 2 * window), 1) & (dec_seq - 1)
    c_c = lax.broadcasted_iota(jnp.int32, (nb, m_rows, 2 * window), 2) & (window - 1)
    valid_c = c_c > t_c
    t_n = lax.broadcasted_iota(jnp.int32, (nb, m_rows, 2 * dec_seq), 1) & (dec_seq - 1)
    col_n = lax.broadcasted_iota(jnp.int32, (nb, m_rows, 2 * dec_seq), 2)
    valid_n = (col_n & (dec_seq - 1)) <= t_n
    first_n = col_n < dec_seq
    lane_low = lax.broadcasted_iota(jnp.int32, (nb, m_rows, LANES), 2) < HEAD_DIM
    q3 = q_s[...].reshape(nb, dec_seq, d)
    bmm_nt = (((2,), (2,)), ((0,), (0,)))
    bmm_nn = (((2,), (1,)), ((0,), (0,)))
    for kh in range(N_KV_HEADS):
        rhs_c = jnp.concatenate([k_old[kh][0], k_old[kh][1]], axis=1)
        vv_c = jnp.concatenate([v_old[kh][0], v_old[kh][1]], axis=1)
        rhs_n = jnp.concatenate([k_new[kh][0].astype(F32), k_new[kh][1].astype(F32)], axis=1).astype(BF16)
        vv_n = jnp.concatenate([v_new[kh][0].astype(F32), v_new[kh][1].astype(F32)], axis=1).astype(BF16)
        qs = jnp.concatenate(
            [q3[:, :, kh * kv_lanes + j * LANES: kh * kv_lanes + (j + 1) * LANES]
             for j in range(PAIRS_PER_KV)], axis=1).astype(BF16)
        s_c = lax.dot_general(qs, rhs_c, bmm_nt, preferred_element_type=F32)
        s_n = lax.dot_general(qs, rhs_n, bmm_nt, preferred_element_type=F32)
        s_c = jnp.where(valid_c, s_c, NEG_INF)
        s_n = jnp.where(valid_n, s_n, NEG_INF)
        sa, sb = s_c[:, :, :window], s_c[:, :, window:]
        sink_a = _sink_column(sinks_ref, kh, 0, dec_seq, (1,))
        sink_b = _sink_column(sinks_ref, kh, 1, dec_seq, (1,))
        na = jnp.max(jnp.where(first_n, s_n, NEG_INF), axis=2, keepdims=True)
        nb_ = jnp.max(jnp.where(first_n, NEG_INF, s_n), axis=2, keepdims=True)
        ma = jnp.maximum(jnp.maximum(jnp.max(sa, axis=2, keepdims=True), na), sink_a)
        mb = jnp.maximum(jnp.maximum(jnp.max(sb, axis=2, keepdims=True), nb_), sink_b)
        pa = jnp.exp(sa - ma)
        pb = jnp.exp(sb - mb)
        pn = jnp.exp(s_n - jnp.where(first_n, ma, mb))
        den_a = (jnp.sum(pa, axis=2, keepdims=True) + jnp.exp(sink_a - ma)
                 + jnp.sum(jnp.where(first_n, pn, 0.0), axis=2, keepdims=True))
        den_b = (jnp.sum(pb, axis=2, keepdims=True) + jnp.exp(sink_b - mb)
                 + jnp.sum(jnp.where(first_n, 0.0, pn), axis=2, keepdims=True))
        p_c = jnp.concatenate([pa, pb], axis=2).astype(BF16)
        o = lax.dot_general(p_c, vv_c, bmm_nn, preferred_element_type=F32)
        o = o + lax.dot_general(pn.astype(BF16), vv_n, bmm_nn, preferred_element_type=F32)
        o = o * jnp.where(lane_low, 1.0 / den_a, 1.0 / den_b)
        for j in range(PAIRS_PER_KV):
            att_s[:, kh * kv_lanes + j * LANES: kh * kv_lanes + (j + 1) * LANES] = (
                o[:, j * dec_seq:(j + 1) * dec_seq, :].reshape(tm, LANES).astype(BF16))


def _attn_kernel(*refs, n_prompt_tiles, tiles_per_batch, dec_seq, convert):
    (sinks_ref, x_ref, gpre_ref, wq_ref, bq_ref, k_ref, v_ref, kprev_ref, vprev_ref,
     ck_ref, cv_ref, wo_ref, bo_ref, gpost_ref) = refs[:14]
    n_c = len(convert)
    conv_srcs = refs[14:14 + n_c]
    xo_ref = refs[14 + n_c]
    conv_dsts = refs[15 + n_c:15 + 2 * n_c]
    q_s, att_s = refs[15 + 2 * n_c:17 + 2 * n_c]
    i = pl.program_id(0)
    _convert_ahead(i, convert, conv_srcs, conv_dsts, refs[17 + 2 * n_c:])

    def project_q():
        h = _rms(x_ref[...], gpre_ref[...]).astype(BF16)
        q = jnp.dot(h, wq_ref[...], preferred_element_type=F32) + bq_ref[...]
        q_s[...] = q * (HEAD_DIM ** -0.5)

    def project_o():
        out = jnp.dot(att_s[...], wo_ref[...], preferred_element_type=F32) + bo_ref[...]
        xo_ref[...] = x_ref[...] + _rms(out, gpost_ref[...])

    @pl.when(i < n_prompt_tiles)
    def _():
        project_q()
        _prompt_attention(i, q_s, att_s, sinks_ref, k_ref[...], kprev_ref[...], v_ref[...], vprev_ref[...],
                          tiles_per_batch)
        project_o()

    @pl.when(i >= n_prompt_tiles)
    def _():
        project_q()
        _sample_attention(q_s, att_s, sinks_ref, k_ref[...], v_ref[...], ck_ref[...], cv_ref[...], dec_seq)
        project_o()


def _attn(x, k, v, ck, cv, sinks, gpre, wq, bq, wo, bo, gpost, *, n_prompt_rows, prompt_batch, dec_seq,
          convert):
    rows, d = x.shape
    tm = ROW_TILE
    npt = n_prompt_rows // tm
    tpb = npt // prompt_batch
    blocks_per_tile = tm // CHUNK
    nb = tm // dec_seq
    window = ck.shape[1]
    conv_in, conv_out, conv_shape, conv_scratch = _convert_pieces(convert, rows // tm)
    kern = functools.partial(_attn_kernel, n_prompt_tiles=npt, tiles_per_batch=tpb, dec_seq=dec_seq,
                             convert=tuple(l for _, l in convert))
    prev_spec = pl.BlockSpec((CHUNK, LANES), lambda i: (jnp.maximum(i * blocks_per_tile - 1, 0), 0))
    cache_spec = pl.BlockSpec((nb, window, LANES), lambda i: (jnp.maximum(i - npt, 0), 0, 0))
    out = pl.pallas_call(
        kern,
        grid=(rows // tm,),
        in_specs=[
            pl.BlockSpec(memory_space=pltpu.SMEM),
            _row_spec(tm, d),
            _const_spec((1, d)),
            _const_spec(wq.shape),
            _const_spec((1, d)),
            _row_spec(tm, LANES), _row_spec(tm, LANES), prev_spec, prev_spec,
            cache_spec, cache_spec,
            _const_spec(wo.shape),
            _const_spec((1, d)),
            _const_spec((1, d)),
        ] + conv_in,
        out_specs=[_row_spec(tm, d)] + conv_out,
        out_shape=[jax.ShapeDtypeStruct((rows, d), F32)] + conv_shape,
        scratch_shapes=[pltpu.VMEM((tm, d), F32), pltpu.VMEM((tm, d), BF16)] + conv_scratch,
        compiler_params=_params(),
        name="swa_attention",
    )(sinks, x, gpre, wq, bq, k, v, k, v, ck, cv, wo, bo, gpost, *[stack for stack, _ in convert])
    return out[0], out[1:]


def _mixing_weights(w_s, b_s, n):
    layers, groups = w_s.shape[:2]
    r = lax.broadcasted_iota(jnp.int32, (CHUNK, CHUNK), 0)
    c = lax.broadcasted_iota(jnp.int32, (CHUNK, CHUNK), 1)
    w_prompt = jnp.where(r >= c, w_s, 0.0)
    reps = CHUNK // n
    pick = (lax.broadcasted_iota(jnp.int32, (CHUNK, n), 0) % n
            == lax.broadcasted_iota(jnp.int32, (CHUNK, n), 1)).astype(F32)
    w_tiled = jnp.einsum('ra,lgab,cb->lgrc', pick, w_s[:, :, :n, :n], pick, precision=lax.Precision.HIGHEST)
    w_sample = jnp.where(jnp.logical_and(r >= c, r // n == c // n), w_tiled, 0.0)
    wmix = jnp.stack([w_prompt, w_sample], axis=1).astype(BF16)
    b_prompt = jnp.repeat(jnp.swapaxes(b_s, 1, 2), LANES, axis=2)
    b_sample = jnp.tile(b_prompt[:, :n], (1, reps, 1))
    bmix = jnp.stack([b_prompt, b_sample], axis=1)
    return wmix.reshape(layers * 2, groups, CHUNK, CHUNK), bmix.reshape(layers * 2, CHUNK, -1)


def kernel(x_prompt, x_sample, cache_k, cache_v, sg_norm_pre, sg_w_in, sg_ln_g, sg_ln_b, sg_w_s, sg_b_s, sg_w_out, sg_norm_post, kv_norm, w_kv, b_kv, sw_norm_pre, sw_w_q, sw_b_q, sw_sinks, sw_w_o, sw_b_o, sw_norm_post, f_norm_pre, f_w_gate, f_w_up, f_w_down, f_norm_post):
    bp, sp, d = x_prompt.shape
    bs, ss, _ = x_sample.shape
    n_a = sg_w_in.shape[0]
    n_b = sw_w_q.shape[0]
    depth = n_a + n_b
    n_p, n_s = bp * sp, bs * ss

    wmix, bmix = _mixing_weights(sg_w_s, sg_b_s, ss)
    row = lambda a: a.reshape(1, -1)
    kvh = cache_k.shape[2]
    ck = cache_k.reshape(bs, cache_k.shape[1], -1).astype(BF16)
    cv = cache_v.reshape(bs, cache_v.shape[1], -1).astype(BF16)

    def mixer_sources(l):
        return [(sg_w_in, l), (sg_w_out, l)]

    def attn_sources(l):
        return [(sw_w_q, l), (sw_w_o, l)]

    def ffn_sources(layer):
        return [(f_w_gate, layer), (f_w_up, layer), (f_w_down, layer)]

    def sources_after(layer, in_ffn):
        if not in_ffn:
            return ffn_sources(layer)
        if layer + 1 == depth:
            return []
        return mixer_sources(layer + 1) if layer + 1 < n_a else attn_sources(layer + 1 - n_a)

    xs = [x_prompt.reshape(n_p, d), x_sample.reshape(n_s, d)]
    v_prompt_rows, v_sample_rows = [], []
    k_all = v_all = y_prompt = y_sample = None
    weights = [stack[l].astype(BF16) for stack, l in (mixer_sources(0) if n_a else attn_sources(0))]
    for layer in range(depth):
        if layer < n_a:
            l = layer
            (x, vsp, vss), weights = _mixer(
                xs, l, row(sg_norm_pre[l]), weights[0], row(sg_ln_g[l]), row(sg_ln_b[l]), wmix, bmix, weights[1],
                row(sg_norm_post[l]), n_prompt_rows=n_p, n_sample_rows=n_s, prompt_batch=bp,
                convert=sources_after(layer, False))
            v_prompt_rows.append(vsp[:bp])
            v_sample_rows.append(vss.reshape(bs, ss, -1))
        else:
            l = layer - n_a
            x, weights = _attn(x, k_all, v_all, ck, cv, sw_sinks[l], row(sw_norm_pre[l]), weights[0],
                               row(sw_b_q[l]), weights[1], row(sw_b_o[l]), row(sw_norm_post[l]),
                               n_prompt_rows=n_p, prompt_batch=bp, dec_seq=ss, convert=sources_after(layer, False))
        ffn_args = (row(f_norm_pre[layer]), *weights, row(f_norm_post[layer]))
        nxt = sources_after(layer, True)
        if layer == n_a - 1:
            (x, k_all, v_all), weights = _ffn(
                x, *ffn_args, kv_weights=(row(kv_norm), w_kv.astype(BF16), row(b_kv)), convert=nxt)
        elif layer == depth - 1:
            (y_prompt, y_sample), weights = _ffn(x, *ffn_args, n_prompt_rows=n_p, convert=nxt)
        else:
            (x,), weights = _ffn(x, *ffn_args, convert=nxt)
        xs = [x]

    def last_window(a):
        return a[:n_p].reshape(bp, sp, -1)[:, sp - CHUNK:].reshape(bp, CHUNK, kvh, -1)

    return (y_prompt.reshape(bp, sp, d), y_sample.reshape(bs, ss, d),
            jnp.stack(v_prompt_rows, axis=0), jnp.stack(v_sample_rows, axis=0),
            last_window(k_all), last_window(v_all),
            k_all[n_p:].reshape(bs, ss, kvh, -1), v_all[n_p:].reshape(bs, ss, kvh, -1))
```

```python
import functools

import jax
import jax.numpy as jnp
from jax import lax
from jax.experimental import pallas as pl
from jax.experimental.pallas import tpu as pltpu

F32 = jnp.float32
BF16 = jnp.bfloat16

RMS_EPS = 1e-6
LN_EPS = 1e-5
CHUNK = 128
LANES = 128
HEAD_DIM = 64
N_KV_HEADS = 2
HEADS_PER_KV = 8
PAIRS_PER_KV = HEADS_PER_KV // 2
ROW_TILE = 512
WIDE_TILE = 1024
FFN_SUB_TILE = 256
MIXER_SUB_TILE = 1024
BF16_SUBLANES = 16
CONVERT_STEPS = 16
VMEM_LIMIT = 56 * 1024 * 1024
NEG_INF = float("-inf")


def _rms(x, g):
    ms = jnp.mean(x * x, axis=-1, keepdims=True)
    return x * lax.rsqrt(ms + RMS_EPS) * g


def _const_spec(shape):
    zeros = (0,) * len(shape)
    return pl.BlockSpec(shape, lambda i: zeros, pipeline_mode=pl.Buffered(1))


HBM_SPEC = pl.BlockSpec(memory_space=pl.ANY)


def _convert_pieces(sources, grid_steps):
    if not sources:
        return [], [], [], []
    assert grid_steps > CONVERT_STEPS, "the last chunk's write-back is waited on the step after it"
    n = len(sources)
    scratch = []
    for stack, _ in sources:
        rows, cols = stack.shape[1] // CONVERT_STEPS, stack.shape[2]
        assert rows * CONVERT_STEPS == stack.shape[1] and rows % BF16_SUBLANES == 0
        scratch += [pltpu.VMEM((rows, cols), F32), pltpu.VMEM((rows, cols), BF16)]
    scratch += [pltpu.SemaphoreType.DMA((n,)), pltpu.SemaphoreType.DMA((n,))]
    out_shape = [jax.ShapeDtypeStruct(stack.shape[1:], BF16) for stack, _ in sources]
    return [HBM_SPEC] * n, [HBM_SPEC] * n, out_shape, scratch


def _convert_ahead(i, layers, srcs, dsts, scratch):
    n = len(layers)
    if n == 0:
        return
    sem_read, sem_write = scratch[2 * n:]

    def read(k, c):
        stage = scratch[2 * k]
        rows = stage.shape[0]
        return pltpu.make_async_copy(srcs[k].at[layers[k], pl.ds(c * rows, rows), :], stage, sem_read.at[k])

    def write(k, c):
        stage = scratch[2 * k + 1]
        rows = stage.shape[0]
        return pltpu.make_async_copy(stage, dsts[k].at[pl.ds(c * rows, rows), :], sem_write.at[k])

    @pl.when(i == 0)
    def _():
        for k in range(n):
            read(k, 0).start()

    @pl.when(jnp.logical_and(i > 0, i <= CONVERT_STEPS))
    def _():
        for k in range(n):
            write(k, i - 1).wait()

    @pl.when(i < CONVERT_STEPS)
    def _():
        for k in range(n):
            read(k, i).wait()
            scratch[2 * k + 1][...] = scratch[2 * k][...].astype(BF16)
            write(k, i).start()

    @pl.when(i + 1 < CONVERT_STEPS)
    def _():
        for k in range(n):
            read(k, i + 1).start()


def _row_spec(tm, width):
    return pl.BlockSpec((tm, width), lambda i: (i, 0))


def _params():
    return pltpu.CompilerParams(dimension_semantics=("arbitrary",), vmem_limit_bytes=VMEM_LIMIT)


def _mixer_kernel(*refs, n_prompt_tiles, split_input, convert):
    n_x = 2 if split_input else 1
    n_c = len(convert)
    gpre_ref, win_ref, lng_ref, lnb_ref, wmix_ref, bmix_ref, wout_ref, gpost_ref = refs[n_x:n_x + 8]
    conv_srcs = refs[n_x + 8:n_x + 8 + n_c]
    xo_ref, vsp_ref, vss_ref = refs[n_x + 8 + n_c:n_x + 11 + n_c]
    conv_dsts = refs[n_x + 11 + n_c:n_x + 11 + 2 * n_c]
    y_s = refs[n_x + 11 + 2 * n_c]
    i = pl.program_id(0)
    tm, width = y_s.shape
    sub = MIXER_SUB_TILE
    _convert_ahead(i, convert, conv_srcs, conv_dsts, refs[n_x + 12 + 2 * n_c:])

    def gelu(z):
        return 0.5 * z * (1.0 + lax.erf(z * (0.5 ** 0.5)))

    def normed_input(rows):
        if split_input:
            x = jnp.where(i < n_prompt_tiles, refs[0][rows, :], refs[1][rows, :])
        else:
            x = refs[0][rows, :]
        return x, _rms(x, gpre_ref[...]).astype(BF16)

    def project_v(h):
        return jnp.dot(h, win_ref[:, width:], preferred_element_type=F32)

    def project_u(h):
        return jnp.dot(h, win_ref[:, :width], preferred_element_type=F32)

    def norm_v(rows, zv):
        v = gelu(zv)
        mu = jnp.mean(v, axis=-1, keepdims=True)
        vc = v - mu
        var = jnp.mean(vc * vc, axis=-1, keepdims=True)
        v = vc * lax.rsqrt(var + LN_EPS) * lng_ref[...] + lnb_ref[...]
        if rows.stop == tm:
            vsp_ref[...] = v[sub - CHUNK:, :]
        vss_ref[rows, :] = v
        return v.astype(BF16)

    def mix_and_project_out(rows, x, u, vb):
        bias = bmix_ref[...]
        for c in range(sub // CHUNK):
            r = slice(c * CHUNK, (c + 1) * CHUNK)
            yr = slice(rows.start + c * CHUNK, rows.start + (c + 1) * CHUNK)
            for g in range(width // LANES):
                cols = slice(g * LANES, (g + 1) * LANES)
                mixed = jnp.dot(wmix_ref[g], vb[r, cols], preferred_element_type=F32)
                y_s[yr, cols] = (u[r, cols] * (mixed + bias[:, cols])).astype(BF16)
        out = jnp.dot(y_s[rows, :], wout_ref[...], preferred_element_type=F32)
        xo_ref[rows, :] = x + _rms(out, gpost_ref[...])

    parts = [slice(s, s + sub) for s in range(0, tm, sub)]
    x, h = normed_input(parts[0])
    ahead = (x, project_v(h), project_u(h))
    for n, rows in enumerate(parts):
        x, zv, zu = ahead
        if n + 1 < len(parts):
            x_next, h_next = normed_input(parts[n + 1])
            zv_next = project_v(h_next)
        vb = norm_v(rows, zv)
        if n + 1 < len(parts):
            ahead = (x_next, zv_next, project_u(h_next))
        u = gelu(zu)
        mix_and_project_out(rows, x, u, vb)


def _mixer(xs, layer, gpre, win, lng, lnb, wmix, bmix, wout, gpost, *, n_prompt_rows, n_sample_rows,
           prompt_batch, convert):
    d = xs[0].shape[1]
    width = wout.shape[0]
    tm = WIDE_TILE
    npt = n_prompt_rows // tm
    tpb = npt // prompt_batch
    groups = wmix.shape[1]
    split_input = len(xs) == 2
    if split_input:
        x_specs = [pl.BlockSpec((tm, d), lambda i: (jnp.minimum(i, npt - 1), 0)),
                   pl.BlockSpec((tm, d), lambda i: (jnp.maximum(i - npt, 0), 0),
                                pipeline_mode=pl.Buffered(1 if n_sample_rows == tm else 2))]
    else:
        x_specs = [_row_spec(tm, d)]
    grid_steps = (n_prompt_rows + n_sample_rows) // tm
    conv_in, conv_out, conv_shape, conv_scratch = _convert_pieces(convert, grid_steps)
    kern = functools.partial(_mixer_kernel, n_prompt_tiles=npt, split_input=split_input,
                             convert=tuple(l for _, l in convert))
    out = pl.pallas_call(
        kern,
        grid=(grid_steps,),
        in_specs=x_specs + [
            _const_spec((1, d)),
            _const_spec(win.shape),
            _const_spec((1, width)),
            _const_spec((1, width)),
            pl.BlockSpec((None, groups, CHUNK, CHUNK), lambda i: (2 * layer + i // npt, 0, 0, 0)),
            pl.BlockSpec((None, CHUNK, width), lambda i: (2 * layer + i // npt, 0, 0)),
            _const_spec(wout.shape),
            _const_spec((1, d)),
        ] + conv_in,
        out_specs=[
            _row_spec(tm, d),
            pl.BlockSpec((None, CHUNK, width), lambda i: (jnp.minimum(i // tpb, prompt_batch), 0, 0)),
            pl.BlockSpec((tm, width), lambda i: (jnp.maximum(i - npt, 0), 0)),
        ] + conv_out,
        out_shape=[
            jax.ShapeDtypeStruct((n_prompt_rows + n_sample_rows, d), F32),
            jax.ShapeDtypeStruct((prompt_batch + 1, CHUNK, width), F32),
            jax.ShapeDtypeStruct((n_sample_rows, width), F32),
        ] + conv_shape,
        scratch_shapes=[pltpu.VMEM((tm, width), BF16)] + conv_scratch,
        compiler_params=_params(),
        name="gmlp_mixer",
    )(*xs, gpre, win, lng, lnb, wmix, bmix, wout, gpost, *[stack for stack, _ in convert])
    return out[:3], out[3:]


def _ffn_kernel(*refs, with_kv, sample_first, convert):
    x_ref, gpre_ref, wg_ref, wu_ref, wd_ref, gpost_ref = refs[:6]
    n_in = 9 if with_kv else 6
    n_out = 3 if with_kv else (2 if sample_first else 1)
    n_c = len(convert)
    conv_srcs = refs[n_in:n_in + n_c]
    outs = refs[n_in + n_c:n_in + n_c + n_out]
    conv_dsts = refs[n_in + n_c + n_out:n_in + 2 * n_c + n_out]
    ys_ref = None
    if with_kv:
        kvg_ref, wkv_ref, bkv_ref = refs[6:9]
        xo_ref, k_ref, v_ref = outs
    elif sample_first:
        xo_ref, ys_ref = outs
    else:
        (xo_ref,) = outs
    i = pl.program_id(0)
    tm = x_ref.shape[0]
    _convert_ahead(i, convert, conv_srcs, conv_dsts, refs[n_in + 2 * n_c + n_out:])

    def project_up(rows):
        x = x_ref[rows, :]
        h = _rms(x, gpre_ref[...]).astype(BF16)
        g = jnp.dot(h, wg_ref[...], preferred_element_type=F32)
        u = jnp.dot(h, wu_ref[...], preferred_element_type=F32)
        return x, g, u

    def gate_and_project_down(rows, x, g, u):
        a = (g * (1.0 / (1.0 + jnp.exp(-g))) * u).astype(BF16)
        out = jnp.dot(a, wd_ref[...], preferred_element_type=F32)
        xo_ref[rows, :] = x + _rms(out, gpost_ref[...])

    def project_kv(rows):
        hk = _rms(xo_ref[rows, :], kvg_ref[...]).astype(BF16)
        kv = jnp.dot(hk, wkv_ref[...], preferred_element_type=F32) + bkv_ref[...]
        half = k_ref.shape[1]
        k_ref[rows, :] = kv[:, :half]
        v_ref[rows, :] = kv[:, half:]

    parts = [slice(s, s + FFN_SUB_TILE) for s in range(0, tm, FFN_SUB_TILE)]
    ahead = project_up(parts[0])
    for n, rows in enumerate(parts):
        x, g, u = ahead
        if n + 1 < len(parts):
            ahead = project_up(parts[n + 1])
        gate_and_project_down(rows, x, g, u)
        if with_kv and n > 0:
            project_kv(parts[n - 1])
    if with_kv:
        project_kv(parts[-1])

    if sample_first:
        @pl.when(i == 0)
        def _():
            ys_ref[...] = xo_ref[...]


def _ffn(x, gpre, wg, wu, wd, gpost, kv_weights=None, n_prompt_rows=None, convert=()):
    rows, d = x.shape
    tm = WIDE_TILE
    with_kv = kv_weights is not None
    sample_first = n_prompt_rows is not None
    weight_specs = [_const_spec((1, d)), _const_spec(wg.shape), _const_spec(wu.shape), _const_spec(wd.shape),
                    _const_spec((1, d))]
    args = [x, gpre, wg, wu, wd, gpost]
    if sample_first:
        npt = n_prompt_rows // tm
        assert rows - n_prompt_rows == tm, "the sample rows must fill exactly one tile"
        in_specs = [pl.BlockSpec((tm, d), lambda i: (jnp.where(i == 0, npt, i - 1), 0))] + weight_specs
        out_specs = [pl.BlockSpec((tm, d), lambda i: (jnp.maximum(i - 1, 0), 0)),
                     pl.BlockSpec((tm, d), lambda i: (0, 0))]
        out_shape = [jax.ShapeDtypeStruct((n_prompt_rows, d), F32), jax.ShapeDtypeStruct((tm, d), F32)]
    else:
        in_specs = [_row_spec(tm, d)] + weight_specs
        out_specs = [_row_spec(tm, d)]
        out_shape = [jax.ShapeDtypeStruct((rows, d), F32)]
    if with_kv:
        kvg, wkv, bkv = kv_weights
        half = wkv.shape[1] // 2
        in_specs += [_const_spec((1, d)), _const_spec((d, 2 * half)), _const_spec((1, 2 * half))]
        out_specs += [_row_spec(tm, half)] * 2
        out_shape += [jax.ShapeDtypeStruct((rows, half), F32)] * 2
        args += [kvg, wkv, bkv]
    conv_in, conv_out, conv_shape, conv_scratch = _convert_pieces(convert, rows // tm)
    n_out = len(out_shape)
    out = pl.pallas_call(
        functools.partial(_ffn_kernel, with_kv=with_kv, sample_first=sample_first,
                          convert=tuple(l for _, l in convert)),
        grid=(rows // tm,),
        in_specs=in_specs + conv_in,
        out_specs=out_specs + conv_out,
        out_shape=out_shape + conv_shape,
        scratch_shapes=conv_scratch,
        compiler_params=_params(),
        name="swiglu_ffn_kv" if with_kv else "swiglu_ffn",
    )(*args, *[stack for stack, _ in convert])
    return out[:n_out], out[n_out:]


def _split_heads(k):
    kr = pltpu.roll(k, HEAD_DIM, k.ndim - 1)
    low = lax.broadcasted_iota(jnp.int32, k.shape, k.ndim - 1) < HEAD_DIM
    zero = jnp.zeros_like(k)
    head0 = (jnp.where(low, k, zero).astype(BF16), jnp.where(low, zero, kr).astype(BF16))
    head1 = (jnp.where(low, kr, zero).astype(BF16), jnp.where(low, zero, k).astype(BF16))
    return head0, head1


def _split_heads_t(v):
    vt = v.T
    vt_swapped = jnp.concatenate([vt[HEAD_DIM:], vt[:HEAD_DIM]], axis=0)
    top = lax.broadcasted_iota(jnp.int32, vt.shape, 0) < HEAD_DIM
    zero = jnp.zeros_like(vt)
    head0 = (jnp.where(top, vt, zero).astype(BF16), jnp.where(top, zero, vt_swapped).astype(BF16))
    head1 = (jnp.where(top, vt_swapped, zero).astype(BF16), jnp.where(top, zero, vt).astype(BF16))
    return head0, head1


def _sink_column(sinks_ref, kh, parity, rows_per_pair, lead):
    pieces = []
    for j in range(PAIRS_PER_KV):
        s = sinks_ref[kh * HEADS_PER_KV + 2 * j + parity]
        pieces.append(jnp.full(lead + (rows_per_pair, 1), s, F32))
    return jnp.concatenate(pieces, axis=len(lead))


def _sink_row(sinks_ref, kh, parity):
    pieces = [jnp.full((1, CHUNK), sinks_ref[kh * HEADS_PER_KV + 2 * j + parity], F32)
              for j in range(PAIRS_PER_KV)]
    return jnp.concatenate(pieces, axis=1)


def _prompt_attention(i, q_s, att_s, sinks_ref, k_own, k_prev, v_own, v_prev, tiles_per_batch):
    tm = q_s.shape[0]
    kv_lanes = PAIRS_PER_KV * LANES
    n_q = PAIRS_PER_KV * CHUNK
    k_own = _split_heads(k_own)
    k_prev = _split_heads(k_prev)
    v_own = _split_heads_t(v_own)
    v_prev = _split_heads_t(v_prev)
    slot = lax.broadcasted_iota(jnp.int32, (2 * CHUNK, n_q), 0) & (CHUNK - 1)
    t = lax.broadcasted_iota(jnp.int32, (2 * CHUNK, n_q), 1) & (CHUNK - 1)
    own = slot <= t
    no_prev = jnp.where((i % tiles_per_batch) == 0, NEG_INF, 0.0)
    top = lax.broadcasted_iota(jnp.int32, (LANES, n_q), 0) < HEAD_DIM

    def scores(blk, kh):
        rows = slice(blk * CHUNK, (blk + 1) * CHUNK)
        prow = slice((blk - 1) * CHUNK, blk * CHUNK)
        ka_o, kb_o = k_own[kh][0][rows], k_own[kh][1][rows]
        ka_p, kb_p = k_prev[kh] if blk == 0 else (k_own[kh][0][prow], k_own[kh][1][prow])
        keys = jnp.concatenate([ka_o, kb_o, ka_p, kb_p], axis=0)
        qs = jnp.concatenate(
            [q_s[rows, kh * kv_lanes + j * LANES: kh * kv_lanes + (j + 1) * LANES]
             for j in range(PAIRS_PER_KV)], axis=0).astype(BF16)
        s4 = lax.dot_general(keys, qs, (((1,), (1,)), ((), ())), preferred_element_type=F32)
        s_prev = s4[2 * CHUNK:]
        if blk == 0:
            s_prev = s_prev + no_prev
        return jnp.where(own, s4[:2 * CHUNK], s_prev)

    def finish(blk, kh, s):
        rows = slice(blk * CHUNK, (blk + 1) * CHUNK)
        prow = slice((blk - 1) * CHUNK, blk * CHUNK)
        va_o, vb_o = v_own[kh][0][:, rows], v_own[kh][1][:, rows]
        va_p, vb_p = v_prev[kh] if blk == 0 else (v_own[kh][0][:, prow], v_own[kh][1][:, prow])
        vals_t = jnp.concatenate([va_o, vb_o, va_p, vb_p], axis=1)
        sa, sb = s[:CHUNK], s[CHUNK:]
        sink_a = _sink_row(sinks_ref, kh, 0)
        sink_b = _sink_row(sinks_ref, kh, 1)
        ma = jnp.maximum(jnp.max(sa, axis=0, keepdims=True), sink_a)
        mb = jnp.maximum(jnp.max(sb, axis=0, keepdims=True), sink_b)
        pa = jnp.exp(sa - ma)
        pb = jnp.exp(sb - mb)
        inv_a = 1.0 / (jnp.sum(pa, axis=0, keepdims=True) + jnp.exp(sink_a - ma))
        inv_b = 1.0 / (jnp.sum(pb, axis=0, keepdims=True) + jnp.exp(sink_b - mb))
        p = jnp.concatenate([pa, pb], axis=0)
        p4 = jnp.concatenate([jnp.where(own, p, 0.0), jnp.where(own, 0.0, p)], axis=0).astype(BF16)
        o_t = jnp.dot(vals_t, p4, preferred_element_type=F32)
        o_t = o_t * jnp.where(top, inv_a, inv_b)
        for j in range(PAIRS_PER_KV):
            att_s[rows, kh * kv_lanes + j * LANES: kh * kv_lanes + (j + 1) * LANES] = (
                o_t[:, j * CHUNK:(j + 1) * CHUNK].T.astype(BF16))

    items = [(blk, kh) for blk in range(tm // CHUNK) for kh in range(N_KV_HEADS)]
    s_next = scores(*items[0])
    for n, item in enumerate(items):
        s_cur = s_next
        if n + 1 < len(items):
            s_next = scores(*items[n + 1])
        finish(*item, s_cur)


def _sample_attention(q_s, att_s, sinks_ref, k_new, v_new, k_old, v_old, dec_seq):
    tm, d = q_s.shape
    kv_lanes = PAIRS_PER_KV * LANES
    nb = tm // dec_seq
    m_rows = PAIRS_PER_KV * dec_seq
    window = k_old.shape[1]
    k_new = _split_heads(k_new.reshape(nb, dec_seq, LANES))
    v_new = _split_heads(v_new.reshape(nb, dec_seq, LANES))
    k_old = _split_heads(k_old.astype(F32))
    v_old = _split_heads(v_old.astype(F32))
    t_c = lax.broadcasted_iota(jnp.int32, (nb, m_rows, 2 * window), 1) & (dec_seq - 1)
    c_c = lax.broadcasted_iota(jnp.int32, (nb, m_rows, 2 * window), 2) & (window - 1)
    valid_c = c_c > t_c
    t_n = lax.broadcasted_iota(jnp.int32, (nb, m_rows, 2 * dec_seq), 1) & (dec_seq - 1)
    col_n = lax.broadcasted_iota(jnp.int32, (nb, m_rows, 2 * dec_seq), 2)
    valid_n = (col_n & (dec_seq - 1)) <= t_n
    first_n = col_n < dec_seq
    lane_low = lax.broadcasted_iota(jnp.int32, (nb, m_rows, LANES), 2) < HEAD_DIM
    q3 = q_s[...].reshape(nb, dec_seq, d)
    bmm_nt = (((2,), (2,)), ((0,), (0,)))
    bmm_nn = (((2,), (1,)), ((0,), (0,)))
    for kh in range(N_KV_HEADS):
        rhs_c = jnp.concatenate([k_old[kh][0], k_old[kh][1]], axis=1)
        vv_c = jnp.concatenate([v_old[kh][0], v_old[kh][1]], axis=1)
        rhs_n = jnp.concatenate([k_new[kh][0].astype(F32), k_new[kh][1].astype(F32)], axis=1).astype(BF16)
        vv_n = jnp.concatenate([v_new[kh][0].astype(F32), v_new[kh][1].astype(F32)], axis=1).astype(BF16)
        qs = jnp.concatenate(
            [q3[:, :, kh * kv_lanes + j * LANES: kh * kv_lanes + (j + 1) * LANES]
             for j in range(PAIRS_PER_KV)], axis=1).astype(BF16)
        s_c = lax.dot_general(qs, rhs_c, bmm_nt, preferred_element_type=F32)
        s_n = lax.dot_general(qs, rhs_n, bmm_nt, preferred_element_type=F32)
        s_c = jnp.where(valid_c, s_c, NEG_INF)
        s_n = jnp.where(valid_n, s_n, NEG_INF)
        sa, sb = s_c[:, :, :window], s_c[:, :, window:]
        sink_a = _sink_column(sinks_ref, kh, 0, dec_seq, (1,))
        sink_b = _sink_column(sinks_ref, kh, 1, dec_seq, (1,))
        na = jnp.max(jnp.where(first_n, s_n, NEG_INF), axis=2, keepdims=True)
        nb_ = jnp.max(jnp.where(first_n, NEG_INF, s_n), axis=2, keepdims=True)
        ma = jnp.maximum(jnp.maximum(jnp.max(sa, axis=2, keepdims=True), na), sink_a)
        mb = jnp.maximum(jnp.maximum(jnp.max(sb, axis=2, keepdims=True), nb_), sink_b)
        pa = jnp.exp(sa - ma)
        pb = jnp.exp(sb - mb)
        pn = jnp.exp(s_n - jnp.where(first_n, ma, mb))
        den_a = (jnp.sum(pa, axis=2, keepdims=True) + jnp.exp(sink_a - ma)
                 + jnp.sum(jnp.where(first_n, pn, 0.0), axis=2, keepdims=True))
        den_b = (jnp.sum(pb, axis=2, keepdims=True) + jnp.exp(sink_b - mb)
                 + jnp.sum(jnp.where(first_n, 0.0, pn), axis=2, keepdims=True))
        p_c = jnp.concatenate([pa, pb], axis=2).astype(BF16)
        o = lax.dot_general(p_c, vv_c, bmm_nn, preferred_element_type=F32)
        o = o + lax.dot_general(pn.astype(BF16), vv_n, bmm_nn, preferred_element_type=F32)
        o = o * jnp.where(lane_low, 1.0 / den_a, 1.0 / den_b)
        for j in range(PAIRS_PER_KV):
            att_s[:, kh * kv_lanes + j * LANES: kh * kv_lanes + (j + 1) * LANES] = (
                o[:, j * dec_seq:(j + 1) * dec_seq, :].reshape(tm, LANES).astype(BF16))


def _attn_kernel(*refs, n_prompt_tiles, tiles_per_batch, dec_seq, convert):
    (sinks_ref, x_ref, gpre_ref, wq_ref, bq_ref, k_ref, v_ref, kprev_ref, vprev_ref,
     ck_ref, cv_ref, wo_ref, bo_ref, gpost_ref) = refs[:14]
    n_c = len(convert)
    conv_srcs = refs[14:14 + n_c]
    xo_ref = refs[14 + n_c]
    conv_dsts = refs[15 + n_c:15 + 2 * n_c]
    q_s, att_s = refs[15 + 2 * n_c:17 + 2 * n_c]
    i = pl.program_id(0)
    _convert_ahead(i, convert, conv_srcs, conv_dsts, refs[17 + 2 * n_c:])

    def project_q():
        h = _rms(x_ref[...], gpre_ref[...]).astype(BF16)
        q = jnp.dot(h, wq_ref[...], preferred_element_type=F32) + bq_ref[...]
        q_s[...] = q * (HEAD_DIM ** -0.5)

    def project_o():
        out = jnp.dot(att_s[...], wo_ref[...], preferred_element_type=F32) + bo_ref[...]
        xo_ref[...] = x_ref[...] + _rms(out, gpost_ref[...])

    @pl.when(i < n_prompt_tiles)
    def _():
        project_q()
        _prompt_attention(i, q_s, att_s, sinks_ref, k_ref[...], kprev_ref[...], v_ref[...], vprev_ref[...],
                          tiles_per_batch)
        project_o()

    @pl.when(i >= n_prompt_tiles)
    def _():
        project_q()
        _sample_attention(q_s, att_s, sinks_ref, k_ref[...], v_ref[...], ck_ref[...], cv_ref[...], dec_seq)
        project_o()


def _attn(x, k, v, ck, cv, sinks, gpre, wq, bq, wo, bo, gpost, *, n_prompt_rows, prompt_batch, dec_seq,
          convert):
    rows, d = x.shape
    tm = ROW_TILE
    npt = n_prompt_rows // tm
    tpb = npt // prompt_batch
    blocks_per_tile = tm // CHUNK
    nb = tm // dec_seq
    window = ck.shape[1]
    conv_in, conv_out, conv_shape, conv_scratch = _convert_pieces(convert, rows // tm)
    kern = functools.partial(_attn_kernel, n_prompt_tiles=npt, tiles_per_batch=tpb, dec_seq=dec_seq,
                             convert=tuple(l for _, l in convert))
    prev_spec = pl.BlockSpec((CHUNK, LANES), lambda i: (jnp.maximum(i * blocks_per_tile - 1, 0), 0))
    cache_spec = pl.BlockSpec((nb, window, LANES), lambda i: (jnp.maximum(i - npt, 0), 0, 0))
    out = pl.pallas_call(
        kern,
        grid=(rows // tm,),
        in_specs=[
            pl.BlockSpec(memory_space=pltpu.SMEM),
            _row_spec(tm, d),
            _const_spec((1, d)),
            _const_spec(wq.shape),
            _const_spec((1, d)),
            _row_spec(tm, LANES), _row_spec(tm, LANES), prev_spec, prev_spec,
            cache_spec, cache_spec,
            _const_spec(wo.shape),
            _const_spec((1, d)),
            _const_spec((1, d)),
        ] + conv_in,
        out_specs=[_row_spec(tm, d)] + conv_out,
        out_shape=[jax.ShapeDtypeStruct((rows, d), F32)] + conv_shape,
        scratch_shapes=[pltpu.VMEM((tm, d), F32), pltpu.VMEM((tm, d), BF16)] + conv_scratch,
        compiler_params=_params(),
        name="swa_attention",
    )(sinks, x, gpre, wq, bq, k, v, k, v, ck, cv, wo, bo, gpost, *[stack for stack, _ in convert])
    return out[0], out[1:]


def _mixing_weights(w_s, b_s, n):
    layers, groups = w_s.shape[:2]
    r = lax.broadcasted_iota(jnp.int32, (CHUNK, CHUNK), 0)
    c = lax.broadcasted_iota(jnp.int32, (CHUNK, CHUNK), 1)
    w_prompt = jnp.where(r >= c, w_s, 0.0)
    reps = CHUNK // n
    pick = (lax.broadcasted_iota(jnp.int32, (CHUNK, n), 0) % n
            == lax.broadcasted_iota(jnp.int32, (CHUNK, n), 1)).astype(F32)
    w_tiled = jnp.einsum('ra,lgab,cb->lgrc', pick, w_s[:, :, :n, :n], pick, precision=lax.Precision.HIGHEST)
    w_sample = jnp.where(jnp.logical_and(r >= c, r // n == c // n), w_tiled, 0.0)
    wmix = jnp.stack([w_prompt, w_sample], axis=1).astype(BF16)
    b_prompt = jnp.repeat(jnp.swapaxes(b_s, 1, 2), LANES, axis=2)
    b_sample = jnp.tile(b_prompt[:, :n], (1, reps, 1))
    bmix = jnp.stack([b_prompt, b_sample], axis=1)
    return wmix.reshape(layers * 2, groups, CHUNK, CHUNK), bmix.reshape(layers * 2, CHUNK, -1)


def kernel(x_prompt, x_sample, cache_k, cache_v, sg_norm_pre, sg_w_in, sg_ln_g, sg_ln_b, sg_w_s, sg_b_s, sg_w_out, sg_norm_post, kv_norm, w_kv, b_kv, sw_norm_pre, sw_w_q, sw_b_q, sw_sinks, sw_w_o, sw_b_o, sw_norm_post, f_norm_pre, f_w_gate, f_w_up, f_w_down, f_norm_post):
    bp, sp, d = x_prompt.shape
    bs, ss, _ = x_sample.shape
    n_a = sg_w_in.shape[0]
    n_b = sw_w_q.shape[0]
    depth = n_a + n_b
    n_p, n_s = bp * sp, bs * ss

    wmix, bmix = _mixing_weights(sg_w_s, sg_b_s, ss)
    row = lambda a: a.reshape(1, -1)
    kvh = cache_k.shape[2]
    ck = cache_k.reshape(bs, cache_k.shape[1], -1).astype(BF16)
    cv = cache_v.reshape(bs, cache_v.shape[1], -1).astype(BF16)

    def mixer_sources(l):
        return [(sg_w_in, l), (sg_w_out, l)]

    def attn_sources(l):
        return [(sw_w_q, l), (sw_w_o, l)]

    def ffn_sources(layer):
        return [(f_w_gate, layer), (f_w_up, layer), (f_w_down, layer)]

    def sources_after(layer, in_ffn):
        if not in_ffn:
            return ffn_sources(layer)
        if layer + 1 == depth:
            return []
        return mixer_sources(layer + 1) if layer + 1 < n_a else attn_sources(layer + 1 - n_a)

    xs = [x_prompt.reshape(n_p, d), x_sample.reshape(n_s, d)]
    v_prompt_rows, v_sample_rows = [], []
    k_all = v_all = y_prompt = y_sample = None
    weights = [stack[l].astype(BF16) for stack, l in (mixer_sources(0) if n_a else attn_sources(0))]
    for layer in range(depth):
        if layer < n_a:
            l = layer
            (x, vsp, vss), weights = _mixer(
                xs, l, row(sg_norm_pre[l]), weights[0], row(sg_ln_g[l]), row(sg_ln_b[l]), wmix, bmix, weights[1],
                row(sg_norm_post[l]), n_prompt_rows=n_p, n_sample_rows=n_s, prompt_batch=bp,
                convert=sources_after(layer, False))
            v_prompt_rows.append(vsp[:bp])
            v_sample_rows.append(vss.reshape(bs, ss, -1))
        else:
            l = layer - n_a
            x, weights = _attn(x, k_all, v_all, ck, cv, sw_sinks[l], row(sw_norm_pre[l]), weights[0],
                               row(sw_b_q[l]), weights[1], row(sw_b_o[l]), row(sw_norm_post[l]),
                               n_prompt_rows=n_p, prompt_batch=bp, dec_seq=ss, convert=sources_after(layer, False))
        ffn_args = (row(f_norm_pre[layer]), *weights, row(f_norm_post[layer]))
        nxt = sources_after(layer, True)
        if layer == n_a - 1:
            (x, k_all, v_all), weights = _ffn(
                x, *ffn_args, kv_weights=(row(kv_norm), w_kv.astype(BF16), row(b_kv)), convert=nxt)
        elif layer == depth - 1:
            (y_prompt, y_sample), weights = _ffn(x, *ffn_args, n_prompt_rows=n_p, convert=nxt)
        else:
            (x,), weights = _ffn(x, *ffn_args, convert=nxt)
        xs = [x]

    def last_window(a):
        rows = [a[(b + 1) * sp - CHUNK:(b + 1) * sp] for b in range(bp)]
        return jnp.stack(rows, axis=0).reshape(bp, CHUNK, kvh, -1)

    return (y_prompt.reshape(bp, sp, d), y_sample.reshape(bs, ss, d),
            jnp.stack(v_prompt_rows, axis=0), jnp.stack(v_sample_rows, axis=0),
            last_window(k_all), last_window(v_all),
            k_all[n_p:].reshape(bs, ss, kvh, -1), v_all[n_p:].reshape(bs, ss, kvh, -1))
```

```python
import functools

import jax
import jax.numpy as jnp
from jax import lax
from jax.experimental import pallas as pl
from jax.experimental.pallas import tpu as pltpu

F32 = jnp.float32
BF16 = jnp.bfloat16

RMS_EPS = 1e-6
LN_EPS = 1e-5
CHUNK = 128
LANES = 128
HEAD_DIM = 64
N_KV_HEADS = 2
HEADS_PER_KV = 8
PAIRS_PER_KV = HEADS_PER_KV // 2
ROW_TILE = 512
WIDE_TILE = 1024
FFN_SUB_TILE = 256
MIXER_SUB_TILE = 1024
BF16_SUBLANES = 16
CONVERT_STEPS = 16
VMEM_LIMIT = 56 * 1024 * 1024
NEG_INF = float("-inf")


def _rms(x, g):
    ms = jnp.mean(x * x, axis=-1, keepdims=True)
    return x * lax.rsqrt(ms + RMS_EPS) * g


def _const_spec(shape):
    zeros = (0,) * len(shape)
    return pl.BlockSpec(shape, lambda i: zeros, pipeline_mode=pl.Buffered(1))


HBM_SPEC = pl.BlockSpec(memory_space=pl.ANY)


def _convert_pieces(sources, grid_steps):
    if not sources:
        return [], [], [], []
    assert grid_steps > CONVERT_STEPS, "the last chunk's write-back is waited on the step after it"
    n = len(sources)
    scratch = []
    for stack, _ in sources:
        rows, cols = stack.shape[1] // CONVERT_STEPS, stack.shape[2]
        assert rows * CONVERT_STEPS == stack.shape[1] and rows % BF16_SUBLANES == 0
        scratch += [pltpu.VMEM((rows, cols), F32), pltpu.VMEM((rows, cols), BF16)]
    scratch += [pltpu.SemaphoreType.DMA((n,)), pltpu.SemaphoreType.DMA((n,))]
    out_shape = [jax.ShapeDtypeStruct(stack.shape[1:], BF16) for stack, _ in sources]
    return [HBM_SPEC] * n, [HBM_SPEC] * n, out_shape, scratch


def _convert_ahead(i, layers, srcs, dsts, scratch):
    n = len(layers)
    if n == 0:
        return
    sem_read, sem_write = scratch[2 * n:]

    def read(k, c):
        stage = scratch[2 * k]
        rows = stage.shape[0]
        return pltpu.make_async_copy(srcs[k].at[layers[k], pl.ds(c * rows, rows), :], stage, sem_read.at[k])

    def write(k, c):
        stage = scratch[2 * k + 1]
        rows = stage.shape[0]
        return pltpu.make_async_copy(stage, dsts[k].at[pl.ds(c * rows, rows), :], sem_write.at[k])

    @pl.when(i == 0)
    def _():
        for k in range(n):
            read(k, 0).start()

    @pl.when(jnp.logical_and(i > 0, i <= CONVERT_STEPS))
    def _():
        for k in range(n):
            write(k, i - 1).wait()

    @pl.when(i < CONVERT_STEPS)
    def _():
        for k in range(n):
            read(k, i).wait()
            scratch[2 * k + 1][...] = scratch[2 * k][...].astype(BF16)
            write(k, i).start()

    @pl.when(i + 1 < CONVERT_STEPS)
    def _():
        for k in range(n):
            read(k, i + 1).start()


def _row_spec(tm, width):
    return pl.BlockSpec((tm, width), lambda i: (i, 0))


def _params():
    return pltpu.CompilerParams(dimension_semantics=("arbitrary",), vmem_limit_bytes=VMEM_LIMIT)


def _mixer_kernel(*refs, n_prompt_tiles, split_input, convert):
    n_x = 2 if split_input else 1
    n_c = len(convert)
    gpre_ref, win_ref, lng_ref, lnb_ref, wmix_ref, bmix_ref, wout_ref, gpost_ref = refs[n_x:n_x + 8]
    conv_srcs = refs[n_x + 8:n_x + 8 + n_c]
    xo_ref, vsp_ref, vss_ref = refs[n_x + 8 + n_c:n_x + 11 + n_c]
    conv_dsts = refs[n_x + 11 + n_c:n_x + 11 + 2 * n_c]
    y_s = refs[n_x + 11 + 2 * n_c]
    i = pl.program_id(0)
    tm, width = y_s.shape
    sub = MIXER_SUB_TILE
    _convert_ahead(i, convert, conv_srcs, conv_dsts, refs[n_x + 12 + 2 * n_c:])

    def gelu(z):
        return 0.5 * z * (1.0 + lax.erf(z * (0.5 ** 0.5)))

    def normed_input(rows):
        if split_input:
            x = jnp.where(i < n_prompt_tiles, refs[0][rows, :], refs[1][rows, :])
        else:
            x = refs[0][rows, :]
        return x, _rms(x, gpre_ref[...]).astype(BF16)

    def project_v(h):
        return jnp.dot(h, win_ref[:, width:], preferred_element_type=F32)

    def project_u(h):
        return jnp.dot(h, win_ref[:, :width], preferred_element_type=F32)

    def norm_v(rows, zv):
        v = gelu(zv)
        mu = jnp.mean(v, axis=-1, keepdims=True)
        vc = v - mu
        var = jnp.mean(vc * vc, axis=-1, keepdims=True)
        v = vc * lax.rsqrt(var + LN_EPS) * lng_ref[...] + lnb_ref[...]
        if rows.stop == tm:
            vsp_ref[...] = v[sub - CHUNK:, :]
        vss_ref[rows, :] = v
        return v.astype(BF16)

    def mix_and_project_out(rows, x, u, vb):
        bias = bmix_ref[...]
        for c in range(sub // CHUNK):
            r = slice(c * CHUNK, (c + 1) * CHUNK)
            yr = slice(rows.start + c * CHUNK, rows.start + (c + 1) * CHUNK)
            for g in range(width // LANES):
                cols = slice(g * LANES, (g + 1) * LANES)
                mixed = jnp.dot(wmix_ref[g], vb[r, cols], preferred_element_type=F32)
                y_s[yr, cols] = (u[r, cols] * (mixed + bias[:, cols])).astype(BF16)
        out = jnp.dot(y_s[rows, :], wout_ref[...], preferred_element_type=F32)
        xo_ref[rows, :] = x + _rms(out, gpost_ref[...])

    parts = [slice(s, s + sub) for s in range(0, tm, sub)]
    x, h = normed_input(parts[0])
    ahead = (x, project_v(h), project_u(h))
    for n, rows in enumerate(parts):
        x, zv, zu = ahead
        if n + 1 < len(parts):
            x_next, h_next = normed_input(parts[n + 1])
            zv_next = project_v(h_next)
        vb = norm_v(rows, zv)
        if n + 1 < len(parts):
            ahead = (x_next, zv_next, project_u(h_next))
        u = gelu(zu)
        mix_and_project_out(rows, x, u, vb)


def _mixer(xs, layer, gpre, win, lng, lnb, wmix, bmix, wout, gpost, *, n_prompt_rows, n_sample_rows,
           prompt_batch, convert):
    d = xs[0].shape[1]
    width = wout.shape[0]
    tm = WIDE_TILE
    npt = n_prompt_rows // tm
    tpb = npt // prompt_batch
    groups = wmix.shape[1]
    split_input = len(xs) == 2
    if split_input:
        x_specs = [pl.BlockSpec((tm, d), lambda i: (jnp.minimum(i, npt - 1), 0)),
                   pl.BlockSpec((tm, d), lambda i: (jnp.maximum(i - npt, 0), 0),
                                pipeline_mode=pl.Buffered(1 if n_sample_rows == tm else 2))]
    else:
        x_specs = [_row_spec(tm, d)]
    grid_steps = (n_prompt_rows + n_sample_rows) // tm
    conv_in, conv_out, conv_shape, conv_scratch = _convert_pieces(convert, grid_steps)
    kern = functools.partial(_mixer_kernel, n_prompt_tiles=npt, split_input=split_input,
                             convert=tuple(l for _, l in convert))
    out = pl.pallas_call(
        kern,
        grid=(grid_steps,),
        in_specs=x_specs + [
            _const_spec((1, d)),
            _const_spec(win.shape),
            _const_spec((1, width)),
            _const_spec((1, width)),
            pl.BlockSpec((None, groups, CHUNK, CHUNK), lambda i: (2 * layer + i // npt, 0, 0, 0)),
            pl.BlockSpec((None, CHUNK, width), lambda i: (2 * layer + i // npt, 0, 0)),
            _const_spec(wout.shape),
            _const_spec((1, d)),
        ] + conv_in,
        out_specs=[
            _row_spec(tm, d),
            pl.BlockSpec((None, CHUNK, width), lambda i: (jnp.minimum(i // tpb, prompt_batch), 0, 0)),
            pl.BlockSpec((tm, width), lambda i: (jnp.maximum(i - npt, 0), 0)),
        ] + conv_out,
        out_shape=[
            jax.ShapeDtypeStruct((n_prompt_rows + n_sample_rows, d), F32),
            jax.ShapeDtypeStruct((prompt_batch + 1, CHUNK, width), F32),
            jax.ShapeDtypeStruct((n_sample_rows, width), F32),
        ] + conv_shape,
        scratch_shapes=[pltpu.VMEM((tm, width), BF16)] + conv_scratch,
        compiler_params=_params(),
        name="gmlp_mixer",
    )(*xs, gpre, win, lng, lnb, wmix, bmix, wout, gpost, *[stack for stack, _ in convert])
    return out[:3], out[3:]


def _ffn_kernel(*refs, with_kv, sample_first, convert):
    x_ref, gpre_ref, wg_ref, wu_ref, wd_ref, gpost_ref = refs[:6]
    n_in = 9 if with_kv else 6
    n_out = 3 if with_kv else (2 if sample_first else 1)
    n_c = len(convert)
    conv_srcs = refs[n_in:n_in + n_c]
    outs = refs[n_in + n_c:n_in + n_c + n_out]
    conv_dsts = refs[n_in + n_c + n_out:n_in + 2 * n_c + n_out]
    ys_ref = None
    if with_kv:
        kvg_ref, wkv_ref, bkv_ref = refs[6:9]
        xo_ref, k_ref, v_ref = outs
    elif sample_first:
        xo_ref, ys_ref = outs
    else:
        (xo_ref,) = outs
    i = pl.program_id(0)
    tm = x_ref.shape[0]
    _convert_ahead(i, convert, conv_srcs, conv_dsts, refs[n_in + 2 * n_c + n_out:])

    def project_up(rows):
        x = x_ref[rows, :]
        h = _rms(x, gpre_ref[...]).astype(BF16)
        g = jnp.dot(h, wg_ref[...], preferred_element_type=F32)
        u = jnp.dot(h, wu_ref[...], preferred_element_type=F32)
        return x, g, u

    def gate_and_project_down(rows, x, g, u):
        a = (g * (1.0 / (1.0 + jnp.exp(-g))) * u).astype(BF16)
        out = jnp.dot(a, wd_ref[...], preferred_element_type=F32)
        xo_ref[rows, :] = x + _rms(out, gpost_ref[...])

    def project_kv(rows):
        hk = _rms(xo_ref[rows, :], kvg_ref[...]).astype(BF16)
        kv = jnp.dot(hk, wkv_ref[...], preferred_element_type=F32) + bkv_ref[...]
        half = k_ref.shape[1]
        k_ref[rows, :] = kv[:, :half]
        v_ref[rows, :] = kv[:, half:]

    parts = [slice(s, s + FFN_SUB_TILE) for s in range(0, tm, FFN_SUB_TILE)]
    ahead = project_up(parts[0])
    for n, rows in enumerate(parts):
        x, g, u = ahead
        if n + 1 < len(parts):
            ahead = project_up(parts[n + 1])
        gate_and_project_down(rows, x, g, u)
        if with_kv and n > 0:
            project_kv(parts[n - 1])
    if with_kv:
        project_kv(parts[-1])

    if sample_first:
        @pl.when(i == 0)
        def _():
            ys_ref[...] = xo_ref[...]


def _ffn(x, gpre, wg, wu, wd, gpost, kv_weights=None, n_prompt_rows=None, convert=()):
    rows, d = x.shape
    tm = WIDE_TILE
    with_kv = kv_weights is not None
    sample_first = n_prompt_rows is not None
    weight_specs = [_const_spec((1, d)), _const_spec(wg.shape), _const_spec(wu.shape), _const_spec(wd.shape),
                    _const_spec((1, d))]
    args = [x, gpre, wg, wu, wd, gpost]
    if sample_first:
        npt = n_prompt_rows // tm
        assert rows - n_prompt_rows == tm, "the sample rows must fill exactly one tile"
        in_specs = [pl.BlockSpec((tm, d), lambda i: (jnp.where(i == 0, npt, i - 1), 0))] + weight_specs
        out_specs = [pl.BlockSpec((tm, d), lambda i: (jnp.maximum(i - 1, 0), 0)),
                     pl.BlockSpec((tm, d), lambda i: (0, 0))]
        out_shape = [jax.ShapeDtypeStruct((n_prompt_rows, d), F32), jax.ShapeDtypeStruct((tm, d), F32)]
    else:
        in_specs = [_row_spec(tm, d)] + weight_specs
        out_specs = [_row_spec(tm, d)]
        out_shape = [jax.ShapeDtypeStruct((rows, d), F32)]
    if with_kv:
        kvg, wkv, bkv = kv_weights
        half = wkv.shape[1] // 2
        in_specs += [_const_spec((1, d)), _const_spec((d, 2 * half)), _const_spec((1, 2 * half))]
        out_specs += [_row_spec(tm, half)] * 2
        out_shape += [jax.ShapeDtypeStruct((rows, half), F32)] * 2
        args += [kvg, wkv, bkv]
    conv_in, conv_out, conv_shape, conv_scratch = _convert_pieces(convert, rows // tm)
    n_out = len(out_shape)
    out = pl.pallas_call(
        functools.partial(_ffn_kernel, with_kv=with_kv, sample_first=sample_first,
                          convert=tuple(l for _, l in convert)),
        grid=(rows // tm,),
        in_specs=in_specs + conv_in,
        out_specs=out_specs + conv_out,
        out_shape=out_shape + conv_shape,
        scratch_shapes=conv_scratch,
        compiler_params=_params(),
        name="swiglu_ffn_kv" if with_kv else "swiglu_ffn",
    )(*args, *[stack for stack, _ in convert])
    return out[:n_out], out[n_out:]


def _split_heads(k):
    kr = pltpu.roll(k, HEAD_DIM, k.ndim - 1)
    low = lax.broadcasted_iota(jnp.int32, k.shape, k.ndim - 1) < HEAD_DIM
    zero = jnp.zeros_like(k)
    head0 = (jnp.where(low, k, zero).astype(BF16), jnp.where(low, zero, kr).astype(BF16))
    head1 = (jnp.where(low, kr, zero).astype(BF16), jnp.where(low, zero, k).astype(BF16))
    return head0, head1


def _split_heads_t(v):
    vt = v.T
    vt_swapped = jnp.concatenate([vt[HEAD_DIM:], vt[:HEAD_DIM]], axis=0)
    top = lax.broadcasted_iota(jnp.int32, vt.shape, 0) < HEAD_DIM
    zero = jnp.zeros_like(vt)
    head0 = (jnp.where(top, vt, zero).astype(BF16), jnp.where(top, zero, vt_swapped).astype(BF16))
    head1 = (jnp.where(top, vt_swapped, zero).astype(BF16), jnp.where(top, zero, vt).astype(BF16))
    return head0, head1


def _sink_column(sinks_ref, kh, parity, rows_per_pair, lead):
    pieces = []
    for j in range(PAIRS_PER_KV):
        s = sinks_ref[kh * HEADS_PER_KV + 2 * j + parity]
        pieces.append(jnp.full(lead + (rows_per_pair, 1), s, F32))
    return jnp.concatenate(pieces, axis=len(lead))


def _sink_row(sinks_ref, kh, parity):
    pieces = [jnp.full((1, CHUNK), sinks_ref[kh * HEADS_PER_KV + 2 * j + parity], F32)
              for j in range(PAIRS_PER_KV)]
    return jnp.concatenate(pieces, axis=1)


def _prompt_attention(i, q_s, att_s, sinks_ref, k_own, k_prev, v_own, v_prev, tiles_per_batch):
    tm = q_s.shape[0]
    kv_lanes = PAIRS_PER_KV * LANES
    n_q = PAIRS_PER_KV * CHUNK
    k_own = _split_heads(k_own)
    k_prev = _split_heads(k_prev)
    v_own = _split_heads_t(v_own)
    v_prev = _split_heads_t(v_prev)
    slot = lax.broadcasted_iota(jnp.int32, (2 * CHUNK, n_q), 0) & (CHUNK - 1)
    t = lax.broadcasted_iota(jnp.int32, (2 * CHUNK, n_q), 1) & (CHUNK - 1)
    own = slot <= t
    no_prev = jnp.where((i % tiles_per_batch) == 0, NEG_INF, 0.0)
    top = lax.broadcasted_iota(jnp.int32, (LANES, n_q), 0) < HEAD_DIM

    def scores(blk, kh):
        rows = slice(blk * CHUNK, (blk + 1) * CHUNK)
        prow = slice((blk - 1) * CHUNK, blk * CHUNK)
        ka_o, kb_o = k_own[kh][0][rows], k_own[kh][1][rows]
        ka_p, kb_p = k_prev[kh] if blk == 0 else (k_own[kh][0][prow], k_own[kh][1][prow])
        keys = jnp.concatenate([ka_o, kb_o, ka_p, kb_p], axis=0)
        qs = jnp.concatenate(
            [q_s[rows, kh * kv_lanes + j * LANES: kh * kv_lanes + (j + 1) * LANES]
             for j in range(PAIRS_PER_KV)], axis=0).astype(BF16)
        s4 = lax.dot_general(keys, qs, (((1,), (1,)), ((), ())), preferred_element_type=F32)
        s_prev = s4[2 * CHUNK:]
        if blk == 0:
            s_prev = s_prev + no_prev
        return jnp.where(own, s4[:2 * CHUNK], s_prev)

    def softmax(kh, s):
        sa, sb = s[:CHUNK], s[CHUNK:]
        sink_a = _sink_row(sinks_ref, kh, 0)
        sink_b = _sink_row(sinks_ref, kh, 1)
        ma = jnp.maximum(jnp.max(sa, axis=0, keepdims=True), sink_a)
        mb = jnp.maximum(jnp.max(sb, axis=0, keepdims=True), sink_b)
        pa = jnp.exp(sa - ma)
        pb = jnp.exp(sb - mb)
        inv_a = 1.0 / (jnp.sum(pa, axis=0, keepdims=True) + jnp.exp(sink_a - ma))
        inv_b = 1.0 / (jnp.sum(pb, axis=0, keepdims=True) + jnp.exp(sink_b - mb))
        p = jnp.concatenate([pa, pb], axis=0)
        p4 = jnp.concatenate([jnp.where(own, p, 0.0), jnp.where(own, 0.0, p)], axis=0).astype(BF16)
        return p4, inv_a, inv_b

    def weigh_values(blk, kh, p4, inv_a, inv_b):
        rows = slice(blk * CHUNK, (blk + 1) * CHUNK)
        prow = slice((blk - 1) * CHUNK, blk * CHUNK)
        va_o, vb_o = v_own[kh][0][:, rows], v_own[kh][1][:, rows]
        va_p, vb_p = v_prev[kh] if blk == 0 else (v_own[kh][0][:, prow], v_own[kh][1][:, prow])
        vals_t = jnp.concatenate([va_o, vb_o, va_p, vb_p], axis=1)
        o_t = jnp.dot(vals_t, p4, preferred_element_type=F32)
        o_t = o_t * jnp.where(top, inv_a, inv_b)
        for j in range(PAIRS_PER_KV):
            att_s[rows, kh * kv_lanes + j * LANES: kh * kv_lanes + (j + 1) * LANES] = (
                o_t[:, j * CHUNK:(j + 1) * CHUNK].T.astype(BF16))

    blocks = list(range(tm // CHUNK))
    s_next = [scores(blocks[0], kh) for kh in range(N_KV_HEADS)]
    for n, blk in enumerate(blocks):
        s_cur = s_next
        if n + 1 < len(blocks):
            s_next = [scores(blocks[n + 1], kh) for kh in range(N_KV_HEADS)]
        probs = [softmax(kh, s_cur[kh]) for kh in range(N_KV_HEADS)]
        for kh in range(N_KV_HEADS):
            weigh_values(blk, kh, *probs[kh])


def _sample_attention(q_s, att_s, sinks_ref, k_new, v_new, k_old, v_old, dec_seq):
    tm, d = q_s.shape
    kv_lanes = PAIRS_PER_KV * LANES
    nb = tm // dec_seq
    m_rows = PAIRS_PER_KV * dec_seq
    window = k_old.shape[1]
    k_new = _split_heads(k_new.reshape(nb, dec_seq, LANES))
    v_new = _split_heads(v_new.reshape(nb, dec_seq, LANES))
    k_old = _split_heads(k_old.astype(F32))
    v_old = _split_heads(v_old.astype(F32))
    t_c = lax.broadcasted_iota(jnp.int32, (nb, m_rows, 2 * window), 1) & (dec_seq - 1)
    c_c = lax.broadcasted_iota(jnp.int32, (nb, m_rows, 2 * window), 2) & (window - 1)
    valid_c = c_c > t_c
    t_n = lax.broadcasted_iota(jnp.int32, (nb, m_rows, 2 * dec_seq), 1) & (dec_seq - 1)
    col_n = lax.broadcasted_iota(jnp.int32, (nb, m_rows, 2 * dec_seq), 2)
    valid_n = (col_n & (dec_seq - 1)) <= t_n
    first_n = col_n < dec_seq
    lane_low = lax.broadcasted_iota(jnp.int32, (nb, m_rows, LANES), 2) < HEAD_DIM
    q3 = q_s[...].reshape(nb, dec_seq, d)
    bmm_nt = (((2,), (2,)), ((0,), (0,)))
    bmm_nn = (((2,), (1,)), ((0,), (0,)))
    for kh in range(N_KV_HEADS):
        rhs_c = jnp.concatenate([k_old[kh][0], k_old[kh][1]], axis=1)
        vv_c = jnp.concatenate([v_old[kh][0], v_old[kh][1]], axis=1)
        rhs_n = jnp.concatenate([k_new[kh][0].astype(F32), k_new[kh][1].astype(F32)], axis=1).astype(BF16)
        vv_n = jnp.concatenate([v_new[kh][0].astype(F32), v_new[kh][1].astype(F32)], axis=1).astype(BF16)
        qs = jnp.concatenate(
            [q3[:, :, kh * kv_lanes + j * LANES: kh * kv_lanes + (j + 1) * LANES]
             for j in range(PAIRS_PER_KV)], axis=1).astype(BF16)
        s_c = lax.dot_general(qs, rhs_c, bmm_nt, preferred_element_type=F32)
        s_n = lax.dot_general(qs, rhs_n, bmm_nt, preferred_element_type=F32)
        s_c = jnp.where(valid_c, s_c, NEG_INF)
        s_n = jnp.where(valid_n, s_n, NEG_INF)
        sa, sb = s_c[:, :, :window], s_c[:, :, window:]
        sink_a = _sink_column(sinks_ref, kh, 0, dec_seq, (1,))
        sink_b = _sink_column(sinks_ref, kh, 1, dec_seq, (1,))
        na = jnp.max(jnp.where(first_n, s_n, NEG_INF), axis=2, keepdims=True)
        nb_ = jnp.max(jnp.where(first_n, NEG_INF, s_n), axis=2, keepdims=True)
        ma = jnp.maximum(jnp.maximum(jnp.max(sa, axis=2, keepdims=True), na), sink_a)
        mb = jnp.maximum(jnp.maximum(jnp.max(sb, axis=2, keepdims=True), nb_), sink_b)
        pa = jnp.exp(sa - ma)
        pb = jnp.exp(sb - mb)
        pn = jnp.exp(s_n - jnp.where(first_n, ma, mb))
        den_a = (jnp.sum(pa, axis=2, keepdims=True) + jnp.exp(sink_a - ma)
                 + jnp.sum(jnp.where(first_n, pn, 0.0), axis=2, keepdims=True))
        den_b = (jnp.sum(pb, axis=2, keepdims=True) + jnp.exp(sink_b - mb)
                 + jnp.sum(jnp.where(first_n, 0.0, pn), axis=2, keepdims=True))
        p_c = jnp.concatenate([pa, pb], axis=2).astype(BF16)
        o = lax.dot_general(p_c, vv_c, bmm_nn, preferred_element_type=F32)
        o = o + lax.dot_general(pn.astype(BF16), vv_n, bmm_nn, preferred_element_type=F32)
        o = o * jnp.where(lane_low, 1.0 / den_a, 1.0 / den_b)
        for j in range(PAIRS_PER_KV):
            att_s[:, kh * kv_lanes + j * LANES: kh * kv_lanes + (j + 1) * LANES] = (
                o[:, j * dec_seq:(j + 1) * dec_seq, :].reshape(tm, LANES).astype(BF16))


def _attn_kernel(*refs, n_prompt_tiles, tiles_per_batch, dec_seq, convert):
    (sinks_ref, x_ref, gpre_ref, wq_ref, bq_ref, k_ref, v_ref, kprev_ref, vprev_ref,
     ck_ref, cv_ref, wo_ref, bo_ref, gpost_ref) = refs[:14]
    n_c = len(convert)
    conv_srcs = refs[14:14 + n_c]
    xo_ref = refs[14 + n_c]
    conv_dsts = refs[15 + n_c:15 + 2 * n_c]
    q_s, att_s = refs[15 + 2 * n_c:17 + 2 * n_c]
    i = pl.program_id(0)
    _convert_ahead(i, convert, conv_srcs, conv_dsts, refs[17 + 2 * n_c:])

    def project_q():
        h = _rms(x_ref[...], gpre_ref[...]).astype(BF16)
        q = jnp.dot(h, wq_ref[...], preferred_element_type=F32) + bq_ref[...]
        q_s[...] = q * (HEAD_DIM ** -0.5)

    def project_o():
        out = jnp.dot(att_s[...], wo_ref[...], preferred_element_type=F32) + bo_ref[...]
        xo_ref[...] = x_ref[...] + _rms(out, gpost_ref[...])

    @pl.when(i < n_prompt_tiles)
    def _():
        project_q()
        _prompt_attention(i, q_s, att_s, sinks_ref, k_ref[...], kprev_ref[...], v_ref[...], vprev_ref[...],
                          tiles_per_batch)
        project_o()

    @pl.when(i >= n_prompt_tiles)
    def _():
        project_q()
        _sample_attention(q_s, att_s, sinks_ref, k_ref[...], v_ref[...], ck_ref[...], cv_ref[...], dec_seq)
        project_o()


def _attn(x, k, v, ck, cv, sinks, gpre, wq, bq, wo, bo, gpost, *, n_prompt_rows, prompt_batch, dec_seq,
          convert):
    rows, d = x.shape
    tm = ROW_TILE
    npt = n_prompt_rows // tm
    tpb = npt // prompt_batch
    blocks_per_tile = tm // CHUNK
    nb = tm // dec_seq
    window = ck.shape[1]
    conv_in, conv_out, conv_shape, conv_scratch = _convert_pieces(convert, rows // tm)
    kern = functools.partial(_attn_kernel, n_prompt_tiles=npt, tiles_per_batch=tpb, dec_seq=dec_seq,
                             convert=tuple(l for _, l in convert))
    prev_spec = pl.BlockSpec((CHUNK, LANES), lambda i: (jnp.maximum(i * blocks_per_tile - 1, 0), 0))
    cache_spec = pl.BlockSpec((nb, window, LANES), lambda i: (jnp.maximum(i - npt, 0), 0, 0))
    out = pl.pallas_call(
        kern,
        grid=(rows // tm,),
        in_specs=[
            pl.BlockSpec(memory_space=pltpu.SMEM),
            _row_spec(tm, d),
            _const_spec((1, d)),
            _const_spec(wq.shape),
            _const_spec((1, d)),
            _row_spec(tm, LANES), _row_spec(tm, LANES), prev_spec, prev_spec,
            cache_spec, cache_spec,
            _const_spec(wo.shape),
            _const_spec((1, d)),
            _const_spec((1, d)),
        ] + conv_in,
        out_specs=[_row_spec(tm, d)] + conv_out,
        out_shape=[jax.ShapeDtypeStruct((rows, d), F32)] + conv_shape,
        scratch_shapes=[pltpu.VMEM((tm, d), F32), pltpu.VMEM((tm, d), BF16)] + conv_scratch,
        compiler_params=_params(),
        name="swa_attention",
    )(sinks, x, gpre, wq, bq, k, v, k, v, ck, cv, wo, bo, gpost, *[stack for stack, _ in convert])
    return out[0], out[1:]


def _mixing_weights(w_s, b_s, n):
    layers, groups = w_s.shape[:2]
    r = lax.broadcasted_iota(jnp.int32, (CHUNK, CHUNK), 0)
    c = lax.broadcasted_iota(jnp.int32, (CHUNK, CHUNK), 1)
    w_prompt = jnp.where(r >= c, w_s, 0.0)
    reps = CHUNK // n
    pick = (lax.broadcasted_iota(jnp.int32, (CHUNK, n), 0) % n
            == lax.broadcasted_iota(jnp.int32, (CHUNK, n), 1)).astype(F32)
    w_tiled = jnp.einsum('ra,lgab,cb->lgrc', pick, w_s[:, :, :n, :n], pick, precision=lax.Precision.HIGHEST)
    w_sample = jnp.where(jnp.logical_and(r >= c, r // n == c // n), w_tiled, 0.0)
    wmix = jnp.stack([w_prompt, w_sample], axis=1).astype(BF16)
    b_prompt = jnp.repeat(jnp.swapaxes(b_s, 1, 2), LANES, axis=2)
    b_sample = jnp.tile(b_prompt[:, :n], (1, reps, 1))
    bmix = jnp.stack([b_prompt, b_sample], axis=1)
    return wmix.reshape(layers * 2, groups, CHUNK, CHUNK), bmix.reshape(layers * 2, CHUNK, -1)


def kernel(x_prompt, x_sample, cache_k, cache_v, sg_norm_pre, sg_w_in, sg_ln_g, sg_ln_b, sg_w_s, sg_b_s, sg_w_out, sg_norm_post, kv_norm, w_kv, b_kv, sw_norm_pre, sw_w_q, sw_b_q, sw_sinks, sw_w_o, sw_b_o, sw_norm_post, f_norm_pre, f_w_gate, f_w_up, f_w_down, f_norm_post):
    bp, sp, d = x_prompt.shape
    bs, ss, _ = x_sample.shape
    n_a = sg_w_in.shape[0]
    n_b = sw_w_q.shape[0]
    depth = n_a + n_b
    n_p, n_s = bp * sp, bs * ss

    wmix, bmix = _mixing_weights(sg_w_s, sg_b_s, ss)
    row = lambda a: a.reshape(1, -1)
    kvh = cache_k.shape[2]
    ck = cache_k.reshape(bs, cache_k.shape[1], -1).astype(BF16)
    cv = cache_v.reshape(bs, cache_v.shape[1], -1).astype(BF16)

    def mixer_sources(l):
        return [(sg_w_in, l), (sg_w_out, l)]

    def attn_sources(l):
        return [(sw_w_q, l), (sw_w_o, l)]

    def ffn_sources(layer):
        return [(f_w_gate, layer), (f_w_up, layer), (f_w_down, layer)]

    def sources_after(layer, in_ffn):
        if not in_ffn:
            return ffn_sources(layer)
        if layer + 1 == depth:
            return []
        return mixer_sources(layer + 1) if layer + 1 < n_a else attn_sources(layer + 1 - n_a)

    xs = [x_prompt.reshape(n_p, d), x_sample.reshape(n_s, d)]
    v_prompt_rows, v_sample_rows = [], []
    k_all = v_all = y_prompt = y_sample = None
    weights = [stack[l].astype(BF16) for stack, l in (mixer_sources(0) if n_a else attn_sources(0))]
    for layer in range(depth):
        if layer < n_a:
            l = layer
            (x, vsp, vss), weights = _mixer(
                xs, l, row(sg_norm_pre[l]), weights[0], row(sg_ln_g[l]), row(sg_ln_b[l]), wmix, bmix, weights[1],
                row(sg_norm_post[l]), n_prompt_rows=n_p, n_sample_rows=n_s, prompt_batch=bp,
                convert=sources_after(layer, False))
            v_prompt_rows.append(vsp[:bp])
            v_sample_rows.append(vss.reshape(bs, ss, -1))
        else:
            l = layer - n_a
            x, weights = _attn(x, k_all, v_all, ck, cv, sw_sinks[l], row(sw_norm_pre[l]), weights[0],
                               row(sw_b_q[l]), weights[1], row(sw_b_o[l]), row(sw_norm_post[l]),
                               n_prompt_rows=n_p, prompt_batch=bp, dec_seq=ss, convert=sources_after(layer, False))
        ffn_args = (row(f_norm_pre[layer]), *weights, row(f_norm_post[layer]))
        nxt = sources_after(layer, True)
        if layer == n_a - 1:
            (x, k_all, v_all), weights = _ffn(
                x, *ffn_args, kv_weights=(row(kv_norm), w_kv.astype(BF16), row(b_kv)), convert=nxt)
        elif layer == depth - 1:
            (y_prompt, y_sample), weights = _ffn(x, *ffn_args, n_prompt_rows=n_p, convert=nxt)
        else:
            (x,), weights = _ffn(x, *ffn_args, convert=nxt)
        xs = [x]

    def last_window(a):
        rows = [a[(b + 1) * sp - CHUNK:(b + 1) * sp] for b in range(bp)]
        return jnp.stack(rows, axis=0).reshape(bp, CHUNK, kvh, -1)

    return (y_prompt.reshape(bp, sp, d), y_sample.reshape(bs, ss, d),
            jnp.stack(v_prompt_rows, axis=0), jnp.stack(v_sample_rows, axis=0),
            last_window(k_all), last_window(v_all),
            k_all[n_p:].reshape(bs, ss, kvh, -1), v_all[n_p:].reshape(bs, ss, kvh, -1))
```
